```python
import math
import jax
import jax.numpy as jnp
from jax import lax
import numpy as np

D_MODEL = 1024
BATCH = 2
SEQ = 8192
DEPTH = 4
DEC_BATCH = 128
DEC_SEQ = 4
PAST_LEN = 2048
PAGE_SIZE = 128

ATT_WIDTH = D_MODEL // 2
N_DIFF_HEADS = 4
DIFF_HD = ATT_WIDTH // (2 * N_DIFF_HEADS)
VAL_HD = 2 * DIFF_HD
SGU_WIDTH = D_MODEL // 4
SGU_GROUPS = 4
CONV_CH = D_MODEL - ATT_WIDTH - SGU_WIDTH
CONV_GROUPS = 4
CONV_K = 31
CHUNK = 128
Q_BLOCK = 128
ROPE_THETA = 10000.0
PEER_HEADS = 8
N_KEYS = 128
N_EXPERTS = N_KEYS * N_KEYS
PEER_TOPK = 16
PEER_QDIM = 128
PEER_BLOCK = 128
EPS = 1e-6
NEG_INF = -1e30
PROJ_SIZES = (2 * N_DIFF_HEADS * DIFF_HD, 2 * N_DIFF_HEADS * DIFF_HD, N_DIFF_HEADS * VAL_HD,
              SGU_WIDTH, SGU_WIDTH, CONV_CH, CONV_CH)
PROJ_WIDTH = sum(PROJ_SIZES)

kernel_name = 'hybrid_diffattn_sgu_conv_peer_step'


def rms_norm(x, g, groups=1):
    shp = x.shape
    xf = x.astype(jnp.float32).reshape(shp[:-1] + (groups, shp[-1] // groups))
    y = xf * lax.rsqrt(jnp.mean(xf * xf, axis=-1, keepdims=True) + EPS)
    return (y.reshape(shp) * g.astype(jnp.float32)).astype(x.dtype)


def split_proj(p):
    idx = np.cumsum(PROJ_SIZES)[:-1].tolist()
    return jnp.split(p, idx, axis=-1)


def rope(x, pos):
    half = x.shape[-1] // 2
    inv = ROPE_THETA ** (-jnp.arange(half, dtype=jnp.float32) / half)
    ang = pos.astype(jnp.float32)[:, None] * inv[None, :]
    shape = (ang.shape[0],) + (1,) * (x.ndim - 3) + (half,)
    cos = jnp.cos(ang).reshape(shape)
    sin = jnp.sin(ang).reshape(shape)
    xf = x.astype(jnp.float32)
    x1, x2 = xf[..., :half], xf[..., half:]
    return jnp.concatenate([x1 * cos - x2 * sin, x2 * cos + x1 * sin], axis=-1).astype(x.dtype)


def diff_attn_prompt(q, k, v, lam):
    B, T = q.shape[:2]
    nb = T // Q_BLOCK
    qb = jnp.moveaxis(q.reshape((B, nb, Q_BLOCK) + q.shape[2:]), 1, 0)
    kpos = jnp.arange(T)
    scale = DIFF_HD ** -0.5

    def one_block(args):
        i, qi = args
        qpos = i * Q_BLOCK + jnp.arange(Q_BLOCK)
        s = jnp.einsum('bqhcd,bkhcd->bhcqk', qi, k).astype(jnp.float32) * scale
        s = jnp.where(kpos[None, :] <= qpos[:, None], s, NEG_INF)
        p = jax.nn.softmax(s, axis=-1)
        a = (p[:, :, 0] - lam * p[:, :, 1]).astype(v.dtype)
        return jnp.einsum('bhqk,bkhd->bqhd', a, v)

    out = lax.map(one_block, (jnp.arange(nb), qb))
    return jnp.moveaxis(out, 0, 1).reshape(B, T, N_DIFF_HEADS, VAL_HD)


def diff_attn_sample(q, k, v, lam, k_past, v_past):
    S = q.shape[1]
    P = k_past.shape[1]
    scale = DIFF_HD ** -0.5
    s_past = jnp.einsum('bqhcd,bkhcd->bhcqk', q, k_past).astype(jnp.float32) * scale
    s_new = jnp.einsum('bqhcd,bkhcd->bhcqk', q, k).astype(jnp.float32) * scale
    causal = jnp.tril(jnp.ones((S, S), dtype=bool))
    s_new = jnp.where(causal, s_new, NEG_INF)
    p = jax.nn.softmax(jnp.concatenate([s_past, s_new], axis=-1), axis=-1)
    a = (p[:, :, 0] - lam * p[:, :, 1]).astype(v.dtype)
    return (jnp.einsum('bhqk,bkhd->bqhd', a[..., :P], v_past)
            + jnp.einsum('bhqk,bkhd->bqhd', a[..., P:], v))


def spatial_gating(su, sv, g, w_s, b_s):
    B, T, C = sv.shape
    cg = C // SGU_GROUPS
    u = jax.nn.gelu(su)
    vn = rms_norm(jax.nn.gelu(sv), g, groups=SGU_GROUPS)
    pad = (-T) % CHUNK
    nc = (T + pad) // CHUNK
    vc = jnp.pad(vn, ((0, 0), (0, pad), (0, 0))).reshape(B, nc, CHUNK, SGU_GROUPS, cg)
    ws = jnp.where(jnp.tril(jnp.ones((CHUNK, CHUNK), dtype=bool)), w_s, 0.0)
    mixed = jnp.einsum('gts,bnsgc->bntgc', ws, vc) + b_s.T[:, :, None]
    mixed = mixed.reshape(B, nc * CHUNK, C)[:, :T]
    return u * mixed, vn


def conv_module(ca, cg, prefix, conv_w, conv_b, g, w_pw):
    C = ca.shape[-1]
    h = ca * jax.nn.sigmoid(cg)
    hp = jnp.concatenate([prefix.astype(h.dtype), h], axis=1)
    y = lax.conv_general_dilated(hp, conv_w[:, None, :].astype(h.dtype), (1,), 'VALID',
                                 dimension_numbers=('NWC', 'WIO', 'NWC'),
                                 feature_group_count=C) + conv_b
    y = jax.nn.silu(rms_norm(y, g, groups=CONV_GROUPS))
    return y @ w_pw, hp[:, -(CONV_K - 1):]


def peer_ffn(xn, wq, keys, u_tab, v_tab):
    B, T, D = xn.shape
    n = B * T
    xf = xn.reshape(n, D)
    q = (xf @ wq).reshape(n, PEER_HEADS, 2, PEER_QDIM // 2)
    s = jnp.einsum('nhcd,hckd->nhck', q, keys).astype(jnp.float32)
    s1, i1 = lax.top_k(s[:, :, 0], PEER_TOPK)
    s2, i2 = lax.top_k(s[:, :, 1], PEER_TOPK)
    n_cand = PEER_TOPK * PEER_TOPK
    cand_s = (s1[..., :, None] + s2[..., None, :]).reshape(n, PEER_HEADS, n_cand)
    cand_i = (i1[..., :, None] * N_KEYS + i2[..., None, :]).reshape(n, PEER_HEADS, n_cand)
    top_s, sel = lax.top_k(cand_s, PEER_TOPK)
    eidx = jnp.take_along_axis(cand_i, sel, axis=-1)
    gates = jax.nn.softmax(top_s, axis=-1).astype(xn.dtype)
    pad = (-n) % PEER_BLOCK
    nb = (n + pad) // PEER_BLOCK
    xb = jnp.pad(xf, ((0, pad), (0, 0))).reshape(nb, PEER_BLOCK, D)
    eb = jnp.pad(eidx, ((0, pad), (0, 0), (0, 0))).reshape(nb, PEER_BLOCK, PEER_HEADS, PEER_TOPK)
    gb = jnp.pad(gates, ((0, pad), (0, 0), (0, 0))).reshape(nb, PEER_BLOCK, PEER_HEADS, PEER_TOPK)

    def one_block(args):
        xi, ei, gi = args
        a = jax.nn.gelu(jnp.einsum('nd,nhkd->nhk', xi, jnp.take(u_tab, ei, axis=0)))
        return jnp.einsum('nhk,nhkd->nd', gi * a, jnp.take(v_tab, ei, axis=0))

    out = lax.map(one_block, (xb, eb, gb)).reshape(nb * PEER_BLOCK, D)[:n]
    return out.reshape(B, T, D)


def setup_inputs(seed: int = 0) -> dict:
    key = jax.random.key(seed)
    keys = iter(jax.random.split(key, 40))

    def nrm(shape, scale):
        return jax.random.normal(next(keys), shape, jnp.float32) * scale

    def gain(shape):
        return 1.0 + nrm(shape, 0.05)

    d = D_MODEL
    n_pages = PAST_LEN // PAGE_SIZE
    n_used = DEC_BATCH * n_pages
    n_phys = n_used + (n_used + 3) // 4
    return {
        'x_prompt': nrm((BATCH, SEQ, d), 1.0),
        'x_sample': nrm((DEC_BATCH, DEC_SEQ, d), 1.0),
        'cache_k': nrm((DEPTH, n_phys, PAGE_SIZE, N_DIFF_HEADS, 2 * DIFF_HD), 1.0),
        'cache_v': nrm((DEPTH, n_phys, PAGE_SIZE, N_DIFF_HEADS, VAL_HD), 1.0),
        'state_conv': nrm((DEPTH, DEC_BATCH, CONV_K - 1, CONV_CH), 0.5),
        'page_table': jax.random.permutation(next(keys), n_phys)[:n_used]
                      .reshape(DEC_BATCH, n_pages).astype(jnp.int32),
        'norm1_g': gain((DEPTH, d)),
        'w_in': nrm((DEPTH, d, PROJ_WIDTH), d ** -0.5),
        'lam_q1': nrm((DEPTH, DIFF_HD), 0.1),
        'lam_k1': nrm((DEPTH, DIFF_HD), 0.1),
        'lam_q2': nrm((DEPTH, DIFF_HD), 0.1),
        'lam_k2': nrm((DEPTH, DIFF_HD), 0.1),
        'diff_norm_g': gain((DEPTH, VAL_HD)),
        'sgu_norm_g': gain((DEPTH, SGU_WIDTH)),
        'w_s': nrm((DEPTH, SGU_GROUPS, CHUNK, CHUNK), CHUNK ** -0.5),
        'b_s': 1.0 + nrm((DEPTH, SGU_GROUPS, CHUNK), 0.1),
        'conv_w': nrm((DEPTH, CONV_K, CONV_CH), CONV_K ** -0.5),
        'conv_b': nrm((DEPTH, CONV_CH), 0.02),
        'conv_norm_g': gain((DEPTH, CONV_CH)),
        'w_conv_pw': nrm((DEPTH, CONV_CH, CONV_CH), CONV_CH ** -0.5),
        'w_out': nrm((DEPTH, d, d), d ** -0.5),
        'norm2_g': gain((DEPTH, d)),
        'peer_wq': nrm((DEPTH, d, PEER_HEADS * PEER_QDIM), d ** -0.5),
        'peer_keys': nrm((DEPTH, PEER_HEADS, 2, N_KEYS, PEER_QDIM // 2), (PEER_QDIM // 2) ** -0.5),
        'peer_u': nrm((DEPTH, N_EXPERTS, d), d ** -0.5),
        'peer_v': nrm((DEPTH, N_EXPERTS, d), 0.2),
        'final_norm_g': gain((d,)),
    }


def reference(x_prompt, x_sample, cache_k, cache_v, state_conv, page_table,
              norm1_g, w_in, lam_q1, lam_k1, lam_q2, lam_k2, diff_norm_g,
              sgu_norm_g, w_s, b_s, conv_w, conv_b, conv_norm_g, w_conv_pw,
              w_out, norm2_g, peer_wq, peer_keys, peer_u, peer_v, final_norm_g):
    db, n_pages = page_table.shape
    past = n_pages * PAGE_SIZE
    pos_p = jnp.arange(x_prompt.shape[1], dtype=jnp.int32)
    pos_s = past + jnp.arange(x_sample.shape[1], dtype=jnp.int32)

    def layer(x, l, pos, attend, conv_prefix):
        B, T, _ = x.shape
        lam_init = 0.8 - 0.6 * math.exp(-0.3 * l)
        lam = (jnp.exp(jnp.sum((lam_q1[l] * lam_k1[l]).astype(jnp.float32)))
               - jnp.exp(jnp.sum((lam_q2[l] * lam_k2[l]).astype(jnp.float32))) + lam_init)
        hn = rms_norm(x, norm1_g[l])
        q, k, v, su, sv, ca, cg = split_proj(hn @ w_in[l])
        q = rope(q.reshape(B, T, N_DIFF_HEADS, 2, DIFF_HD), pos)
        k = rope(k.reshape(B, T, N_DIFF_HEADS, 2, DIFF_HD), pos)
        v = v.reshape(B, T, N_DIFF_HEADS, VAL_HD)
        o_att = rms_norm(attend(q, k, v, lam), diff_norm_g[l]) * (1.0 - lam_init)
        o_sgu, sgu_rows = spatial_gating(su, sv, sgu_norm_g[l], w_s[l], b_s[l])
        o_conv, conv_rows = conv_module(ca, cg, conv_prefix, conv_w[l], conv_b[l],
                                        conv_norm_g[l], w_conv_pw[l])
        mix = jnp.concatenate([o_att.reshape(B, T, ATT_WIDTH), o_sgu, o_conv], axis=-1)
        x = x + mix @ w_out[l]
        x = x + peer_ffn(rms_norm(x, norm2_g[l]), peer_wq[l], peer_keys[l], peer_u[l], peer_v[l])
        return x, k.reshape(B, T, N_DIFF_HEADS, 2 * DIFF_HD), v, conv_rows, sgu_rows

    xp, xs = x_prompt, x_sample
    kp_l, vp_l, ks_l, vs_l, cp_l, cs_l, gs_l = [], [], [], [], [], [], []
    for l in range(DEPTH):
        zeros_prefix = jnp.zeros((xp.shape[0], CONV_K - 1, CONV_CH), xp.dtype)
        xp, kp, vp, cp, _ = layer(xp, l, pos_p, diff_attn_prompt, zeros_prefix)
        k_past = cache_k[l, page_table].reshape(db, past, N_DIFF_HEADS, 2, DIFF_HD)
        v_past = cache_v[l, page_table].reshape(db, past, N_DIFF_HEADS, VAL_HD)
        attend_s = lambda q, k, v, lam: diff_attn_sample(q, k, v, lam, k_past, v_past)
        xs, ks, vs, cs, gs = layer(xs, l, pos_s, attend_s, state_conv[l])
        kp_l.append(kp); vp_l.append(vp); ks_l.append(ks); vs_l.append(vs)
        cp_l.append(cp); cs_l.append(cs); gs_l.append(gs)

    y_prompt = rms_norm(xp, final_norm_g)
    y_sample = rms_norm(xs, final_norm_g)
    new_k_prompt = jnp.stack(kp_l)
    new_v_prompt = jnp.stack(vp_l)
    new_k_sample = jnp.stack(ks_l)
    new_v_sample = jnp.stack(vs_l)
    conv_prompt = jnp.stack(cp_l)
    conv_sample = jnp.stack(cs_l)
    sgu_v_sample = jnp.stack(gs_l)
    return (y_prompt, y_sample, new_k_prompt, new_v_prompt, new_k_sample, new_v_sample,
            conv_prompt, conv_sample, sgu_v_sample)
```

```python
import functools
import math

import numpy as np
import jax
import jax.numpy as jnp
from jax import lax
from jax.experimental import pallas as pl
from jax.experimental.pallas import tpu as pltpu

F32 = jnp.float32
BF16 = jnp.bfloat16

LANES = 128
SUBLANES = 8
EPS = 1e-6
NEG_INF = -1e30
ROPE_THETA = 10000.0

N_HEADS = 4
HEAD_W = 128
SGU_GROUPS = 4
CONV_GROUPS = 4
CONV_K = 31
CHUNK = 128
PAGE = 128
PEER_HEADS = 8
N_KEYS = 128
TOPK = 16

VMEM_LIMIT = 56 * 1024 * 1024


def _cparams(sem):
    return pltpu.CompilerParams(dimension_semantics=sem, vmem_limit_bytes=VMEM_LIMIT)


def _full(shape):
    n = len(shape)
    return pl.BlockSpec(shape, lambda *_: (0,) * n)


def _sigmoid(x):
    return 1.0 / (1.0 + jnp.exp(-x))


def _rms_rows(x, g):
    return x * lax.rsqrt(jnp.mean(x * x, axis=-1, keepdims=True) + EPS) * g


def _group_mean_sq(x, bd, group):
    x2 = x * x
    hi = x2.astype(BF16)
    lo = (x2 - hi.astype(F32)).astype(BF16)
    s = jnp.dot(hi, bd, preferred_element_type=F32) + jnp.dot(lo, bd, preferred_element_type=F32)
    return s * (1.0 / group)


def _rope_block(xh, cos, sin, lo_half):
    sw = jnp.where(lo_half, pltpu.roll(xh, LANES - 32, 1), pltpu.roll(xh, 32, 1))
    return xh * cos + sw * sin


def _lam_from(lam4):
    a = jnp.sum(lam4[0:1, :] * lam4[1:2, :], axis=-1, keepdims=True)
    b = jnp.sum(lam4[2:3, :] * lam4[3:4, :], axis=-1, keepdims=True)
    return jnp.exp(a) - jnp.exp(b)


def _proj_p_kernel(x_ref, g1_ref, win_ref, cos_ref, sin_ref, sgug_ref, ws_ref, bs_ref,
                   cw_ref, cb_ref, cg_ref, wpw_ref, bd_ref,
                   qa_ref, qb_ref, kf_ref, kb_ref, vf_ref, vb_ref, osgu_ref, oconv_ref, crow_ref,
                   hbuf):
    t = pl.program_id(1)
    tm = x_ref.shape[0]
    x = x_ref[...]
    hn = _rms_rows(x, g1_ref[...]).astype(BF16)
    p = jnp.dot(hn, win_ref[...], preferred_element_type=F32)
    aw = N_HEADS * HEAD_W
    q = p[:, 0:aw]
    k = p[:, aw:2 * aw]
    v = p[:, 2 * aw:3 * aw]
    o = 3 * aw
    cw = osgu_ref.shape[1]
    su = p[:, o:o + cw]
    sv = p[:, o + cw:o + 2 * cw]
    ca = p[:, o + 2 * cw:o + 3 * cw]
    cg = p[:, o + 3 * cw:o + 4 * cw]

    cos = cos_ref[...]
    sin = sin_ref[...]
    lane = lax.broadcasted_iota(jnp.int32, (tm, LANES), 1)
    lo_half = (lane & 32) == 0
    first = (lane & 64) == 0
    for h in range(N_HEADS):
        sl = slice(h * HEAD_W, (h + 1) * HEAD_W)
        qr = _rope_block(q[:, sl], cos, sin, lo_half) * 0.125
        qa_ref[:, sl] = jnp.where(first, qr, 0.0).astype(BF16)
        qb_ref[:, sl] = jnp.where(first, 0.0, qr).astype(BF16)
        kr = _rope_block(k[:, sl], cos, sin, lo_half)
        kf_ref[:, sl] = kr
        kb_ref[:, sl] = kr.astype(BF16)
    vf_ref[...] = v
    vb_ref[...] = v.astype(BF16)

    bd = bd_ref[...]
    u = jax.nn.gelu(su)
    vg = jax.nn.gelu(sv)
    vn = vg * lax.rsqrt(_group_mean_sq(vg, bd, cw // SGU_GROUPS) + EPS) * sgug_ref[...]
    vnb = vn.astype(BF16)
    r_i = lax.broadcasted_iota(jnp.int32, (CHUNK, CHUNK), 0)
    c_i = lax.broadcasted_iota(jnp.int32, (CHUNK, CHUNK), 1)
    tril = r_i >= c_i
    ws = [jnp.where(tril, ws_ref[g], 0.0).astype(BF16) for g in range(SGU_GROUPS)]
    left = c_i < 64
    bias = bs_ref[...]
    for c in range(tm // CHUNK):
        rows = slice(c * CHUNK, (c + 1) * CHUNK)
        parts = []
        for gp in range(SGU_GROUPS // 2):
            r = vnb[rows, gp * LANES:(gp + 1) * LANES]
            parts.append(jnp.where(left,
                                   jnp.dot(ws[2 * gp], r, preferred_element_type=F32),
                                   jnp.dot(ws[2 * gp + 1], r, preferred_element_type=F32)))
        mixed = jnp.concatenate(parts, axis=1) + bias
        osgu_ref[rows, :] = (u[rows, :] * mixed).astype(BF16)

    hg = ca * _sigmoid(cg)

    @pl.when(t == 0)
    def _():
        hbuf[0:32, :] = jnp.zeros((32, cw), F32)

    hbuf[32:32 + tm, :] = hg
    y = jnp.zeros((tm, cw), F32) + cb_ref[...]
    for j in range(CONV_K):
        y = y + hbuf[pl.ds(2 + j, tm), :] * cw_ref[j:j + 1, :]
    yn = y * lax.rsqrt(_group_mean_sq(y, bd, cw // CONV_GROUPS) + EPS) * cg_ref[...]
    ys = yn * _sigmoid(yn)
    oconv_ref[...] = jnp.dot(ys.astype(BF16), wpw_ref[...], preferred_element_type=F32).astype(BF16)
    crow_ref[...] = hbuf[pl.ds(tm + 2, CONV_K - 1), :]
    hbuf[0:32, :] = hbuf[tm:tm + 32, :]


def _proj_prompt(x, g1, win_b, cos, sin, sgug, ws, bs_full, cw, cb, cg, wpw_b, bd, tm):
    B, T, D = x.shape
    PW = win_b.shape[1]
    aw = N_HEADS * HEAD_W
    cwid = sgug.shape[1]
    row = lambda w: pl.BlockSpec((None, tm, w), lambda b, t: (b, t, 0))
    outs = (
        jax.ShapeDtypeStruct((B, T, aw), BF16), jax.ShapeDtypeStruct((B, T, aw), BF16),
        jax.ShapeDtypeStruct((B, T, aw), F32), jax.ShapeDtypeStruct((B, T, aw), BF16),
        jax.ShapeDtypeStruct((B, T, aw), F32), jax.ShapeDtypeStruct((B, T, aw), BF16),
        jax.ShapeDtypeStruct((B, T, cwid), BF16), jax.ShapeDtypeStruct((B, T, cwid), BF16),
        jax.ShapeDtypeStruct((B, CONV_K - 1, cwid), F32),
    )
    return pl.pallas_call(
        _proj_p_kernel,
        out_shape=outs,
        grid=(B, T // tm),
        in_specs=[row(D), _full((1, D)), _full((D, PW)),
                  pl.BlockSpec((tm, LANES), lambda b, t: (t, 0)),
                  pl.BlockSpec((tm, LANES), lambda b, t: (t, 0)),
                  _full((1, cwid)), _full(ws.shape), _full(bs_full.shape),
                  _full(cw.shape), _full((1, cwid)), _full((1, cwid)), _full(wpw_b.shape), _full(bd.shape)],
        out_specs=(row(aw), row(aw), row(aw), row(aw), row(aw), row(aw), row(cwid), row(cwid),
                   pl.BlockSpec((None, CONV_K - 1, cwid), lambda b, t: (b, 0, 0))),
        scratch_shapes=[pltpu.VMEM((tm + 32, cwid), F32)],
        compiler_params=_cparams(("arbitrary", "arbitrary")),
        name="proj_prompt",
    )(x, g1, win_b, cos, sin, sgug, ws, bs_full, cw, cb, cg, wpw_b, bd)


def _proj_s_kernel(x_ref, g1_ref, win_ref, cos_ref, sin_ref, sgug_ref, wsx_ref, bsx_ref,
                   pre_ref, cw_ref, cb_ref, cg_ref, wpw_ref, bd_ref,
                   q_ref, kf_ref, vf_ref, osgu_ref, oconv_ref, h_ref, vn_ref):
    n = x_ref.shape[0]
    nb = pre_ref.shape[1]
    S = n // nb
    x = x_ref[...]
    hn = _rms_rows(x, g1_ref[...]).astype(BF16)
    p = jnp.dot(hn, win_ref[...], preferred_element_type=F32)
    aw = N_HEADS * HEAD_W
    q = p[:, 0:aw]
    k = p[:, aw:2 * aw]
    o = 3 * aw
    cw = osgu_ref.shape[1]
    su = p[:, o:o + cw]
    sv = p[:, o + cw:o + 2 * cw]
    ca = p[:, o + 2 * cw:o + 3 * cw]
    cg = p[:, o + 3 * cw:o + 4 * cw]
    cos = cos_ref[...]
    sin = sin_ref[...]
    lane = lax.broadcasted_iota(jnp.int32, (n, LANES), 1)
    lo_half = (lane & 32) == 0
    for h in range(N_HEADS):
        sl = slice(h * HEAD_W, (h + 1) * HEAD_W)
        q_ref[:, sl] = _rope_block(q[:, sl], cos, sin, lo_half) * 0.125
        kf_ref[:, sl] = _rope_block(k[:, sl], cos, sin, lo_half)
    vf_ref[...] = p[:, 2 * aw:3 * aw]

    bd = bd_ref[...]
    u = jax.nn.gelu(su)
    vg = jax.nn.gelu(sv)
    vn = vg * lax.rsqrt(_group_mean_sq(vg, bd, cw // SGU_GROUPS) + EPS) * sgug_ref[...]
    vn_ref[...] = vn
    for t in range(S):
        mixed = jnp.zeros((nb, cw), F32) + bsx_ref[t:t + 1, :]
        for s in range(t + 1):
            mixed = mixed + wsx_ref[t, s:s + 1, :] * vn[s * nb:(s + 1) * nb, :]
        osgu_ref[t * nb:(t + 1) * nb, :] = (u[t * nb:(t + 1) * nb, :] * mixed).astype(BF16)

    hg = ca * _sigmoid(cg)
    h_ref[...] = hg
    npre = CONV_K - 1
    ys = []
    for t in range(S):
        y = jnp.zeros((nb, cw), F32) + cb_ref[...]
        for j in range(CONV_K):
            kk = t + j
            src = pre_ref[kk] if kk < npre else hg[(kk - npre) * nb:(kk - npre + 1) * nb, :]
            y = y + src * cw_ref[j:j + 1, :]
        ys.append(y)
    y = jnp.concatenate(ys, axis=0)
    yn = y * lax.rsqrt(_group_mean_sq(y, bd, cw // CONV_GROUPS) + EPS) * cg_ref[...]
    ysw = yn * _sigmoid(yn)
    oconv_ref[...] = jnp.dot(ysw.astype(BF16), wpw_ref[...], preferred_element_type=F32).astype(BF16)


def _proj_sample(x, g1, win_b, cos, sin, sgug, wsx, bsx, prefix, cw, cb, cg, wpw_b, bd):
    n, D = x.shape
    aw = N_HEADS * HEAD_W
    cwid = sgug.shape[1]
    args = (x, g1, win_b, cos, sin, sgug, wsx, bsx, prefix, cw, cb, cg, wpw_b, bd)
    outs = (jax.ShapeDtypeStruct((n, aw), F32), jax.ShapeDtypeStruct((n, aw), F32),
            jax.ShapeDtypeStruct((n, aw), F32),
            jax.ShapeDtypeStruct((n, cwid), BF16), jax.ShapeDtypeStruct((n, cwid), BF16),
            jax.ShapeDtypeStruct((n, cwid), F32), jax.ShapeDtypeStruct((n, cwid), F32))
    return pl.pallas_call(
        _proj_s_kernel,
        out_shape=outs,
        grid=(1,),
        in_specs=[_full(a.shape) for a in args],
        out_specs=tuple(_full(o.shape) for o in outs),
        compiler_params=_cparams(("arbitrary",)),
        name="proj_sample",
    )(*args)


def _attn_p_kernel(qt_ref, kt_ref, lam4_ref, g_ref, qa_ref, qb_ref, k_ref, v_ref, o_ref,
                   m0, l0, a0, m1, l1, a1, *, lam_init):
    s_id = pl.program_id(2)
    qi = qt_ref[s_id]
    ki = kt_ref[s_id]
    tq = qa_ref.shape[0]
    tk = k_ref.shape[0]

    @pl.when(ki == 0)
    def _():
        for m, l, a in ((m0, l0, a0), (m1, l1, a1)):
            m[...] = jnp.full(m.shape, NEG_INF, F32)
            l[...] = jnp.zeros(l.shape, F32)
            a[...] = jnp.zeros(a.shape, F32)

    kk = k_ref[...]
    vv = v_ref[...]
    qpos = qi * tq + lax.broadcasted_iota(jnp.int32, (tq, tk), 0)
    kpos = ki * tk + lax.broadcasted_iota(jnp.int32, (tq, tk), 1)
    keep = kpos <= qpos
    for q_ref, m, l, a in ((qa_ref, m0, l0, a0), (qb_ref, m1, l1, a1)):
        s = lax.dot_general(q_ref[...], kk, (((1,), (1,)), ((), ())), preferred_element_type=F32)
        s = jnp.where(keep, s, NEG_INF)
        m_old = m[...]
        m_new = jnp.maximum(m_old, jnp.max(s, axis=-1, keepdims=True))
        alpha = jnp.exp(m_old - m_new)
        p = jnp.exp(s - m_new)
        l[...] = alpha * l[...] + jnp.sum(p, axis=-1, keepdims=True)
        a[...] = alpha * a[...] + jnp.dot(p.astype(BF16), vv, preferred_element_type=F32)
        m[...] = m_new

    @pl.when(ki == (qi * tq + tq - 1) // tk)
    def _():
        lam = _lam_from(lam4_ref[...]) + lam_init
        o = a0[...] / l0[...] - lam * (a1[...] / l1[...])
        o_ref[...] = (_rms_rows(o, g_ref[...]) * (1.0 - lam_init)).astype(o_ref.dtype)


def _attn_prompt(qa, qb, kb, vb, lam4, g, lam_init, tq, tk):
    B, T, aw = qa.shape
    nq = T // tq
    qs, ks = [], []
    for i in range(nq):
        last = (i * tq + tq - 1) // tk
        for j in range(last + 1):
            qs.append(i)
            ks.append(j)
    qt = jnp.asarray(np.array(qs, np.int32))
    kt = jnp.asarray(np.array(ks, np.int32))
    qspec = pl.BlockSpec((None, tq, HEAD_W), lambda b, h, s, qt, kt: (b, qt[s], h))
    kspec = pl.BlockSpec((None, tk, HEAD_W), lambda b, h, s, qt, kt: (b, kt[s], h))
    grid_spec = pltpu.PrefetchScalarGridSpec(
        num_scalar_prefetch=2,
        grid=(B, N_HEADS, len(qs)),
        in_specs=[pl.BlockSpec(lam4.shape, lambda b, h, s, qt, kt: (0, 0)),
                  pl.BlockSpec(g.shape, lambda b, h, s, qt, kt: (0, 0)),
                  qspec, qspec, kspec, kspec],
        out_specs=qspec,
        scratch_shapes=[pltpu.VMEM((tq, 1), F32), pltpu.VMEM((tq, 1), F32), pltpu.VMEM((tq, HEAD_W), F32),
                        pltpu.VMEM((tq, 1), F32), pltpu.VMEM((tq, 1), F32), pltpu.VMEM((tq, HEAD_W), F32)],
    )
    return pl.pallas_call(
        functools.partial(_attn_p_kernel, lam_init=lam_init),
        out_shape=jax.ShapeDtypeStruct((B, T, aw), BF16),
        grid_spec=grid_spec,
        compiler_params=_cparams(("arbitrary", "arbitrary", "arbitrary")),
        name="attn_prompt",
    )(qt, kt, lam4, g, qa, qb, kb, vb)


def _attn_s_kernel(pt_ref, lam4_ref, g_ref, q_ref, kn_ref, vn_ref, *rest, lam_init, ppb, n_new):
    k_refs = rest[:ppb]
    v_refs = rest[ppb:2 * ppb]
    o_ref = rest[2 * ppb]
    m_sc, l_sc, a_sc = rest[2 * ppb + 1:]
    j = pl.program_id(1)
    nj = pl.num_programs(1)

    @pl.when(j == 0)
    def _():
        m_sc[...] = jnp.full(m_sc.shape, NEG_INF, F32)
        l_sc[...] = jnp.zeros(l_sc.shape, F32)
        a_sc[...] = jnp.zeros(a_sc.shape, F32)

    q = q_ref[...]
    qb = q.astype(BF16)
    for kr, vr in zip(k_refs, v_refs):
        kk = kr[...].astype(BF16)
        vv = vr[...].astype(BF16)
        s = lax.dot_general(qb, kk, (((1,), (1,)), ((), ())), preferred_element_type=F32)
        m_old = m_sc[...]
        m_new = jnp.maximum(m_old, jnp.max(s, axis=-1, keepdims=True))
        alpha = jnp.exp(m_old - m_new)
        p = jnp.exp(s - m_new)
        l_sc[...] = alpha * l_sc[...] + jnp.sum(p, axis=-1, keepdims=True)
        a_sc[...] = alpha * a_sc[...] + jnp.dot(p.astype(BF16), vv, preferred_element_type=F32)
        m_sc[...] = m_new

    @pl.when(j == nj - 1)
    def _():
        rows = q.shape[0]
        tok = lax.broadcasted_iota(jnp.int32, (rows, 1), 0) & (n_new - 1)
        sn = []
        for c in range(n_new):
            sc = jnp.sum(q * kn_ref[c:c + 1, :], axis=-1, keepdims=True)
            sn.append(jnp.where(tok >= c, sc, NEG_INF))
        m_old = m_sc[...]
        m_new = m_old
        for sc in sn:
            m_new = jnp.maximum(m_new, sc)
        alpha = jnp.exp(m_old - m_new)
        l = alpha * l_sc[...]
        acc = alpha * a_sc[...]
        for c, sc in enumerate(sn):
            pc = jnp.exp(sc - m_new)
            l = l + pc
            acc = acc + pc * vn_ref[c:c + 1, :]
        xo = acc / l
        lam = _lam_from(lam4_ref[...]) + lam_init
        for h in range(N_HEADS):
            blk = xo[h * 8:(h + 1) * 8, h * HEAD_W:(h + 1) * HEAD_W]
            o8 = blk - lam * pltpu.roll(blk, n_new, 0)
            o_ref[:, h * HEAD_W:(h + 1) * HEAD_W] = _rms_rows(o8, g_ref[...]) * (1.0 - lam_init)


def _attn_sample(pt_flat, q32, kn8, vn8, cache_k2, cache_v2, lam4, g, lam_init, layer_base, n_pages, ppb):
    nb = q32.shape[0]
    aw = q32.shape[2]
    n_new = 4
    steps = n_pages // ppb

    def page_spec(p):
        return pl.BlockSpec((None, PAGE, aw),
                            lambda b, j, pt, p=p: (layer_base + pt[b * n_pages + j * ppb + p], 0, 0))

    grid_spec = pltpu.PrefetchScalarGridSpec(
        num_scalar_prefetch=1,
        grid=(nb, steps),
        in_specs=[pl.BlockSpec(lam4.shape, lambda b, j, pt: (0, 0)),
                  pl.BlockSpec(g.shape, lambda b, j, pt: (0, 0)),
                  pl.BlockSpec((None, 32, aw), lambda b, j, pt: (b, 0, 0)),
                  pl.BlockSpec((None, 8, aw), lambda b, j, pt: (b, 0, 0)),
                  pl.BlockSpec((None, 8, aw), lambda b, j, pt: (b, 0, 0))]
                 + [page_spec(p) for p in range(ppb)] + [page_spec(p) for p in range(ppb)],
        out_specs=pl.BlockSpec((None, 8, aw), lambda b, j, pt: (b, 0, 0)),
        scratch_shapes=[pltpu.VMEM((32, 1), F32), pltpu.VMEM((32, 1), F32), pltpu.VMEM((32, aw), F32)],
    )
    return pl.pallas_call(
        functools.partial(_attn_s_kernel, lam_init=lam_init, ppb=ppb, n_new=n_new),
        out_shape=jax.ShapeDtypeStruct((nb, 8, aw), F32),
        grid_spec=grid_spec,
        compiler_params=_cparams(("arbitrary", "arbitrary")),
        name="attn_sample",
    )(pt_flat, lam4, g, q32, kn8, vn8, *([cache_k2] * ppb), *([cache_v2] * ppb))


def _post_kernel(x_ref, oa_ref, os_ref, oc_ref, wout_ref, g2_ref, wq_ref, x1_ref, xn_ref, qp_ref):
    aw = oa_ref.shape[1]
    cw = os_ref.shape[1]
    x1 = (x_ref[...]
          + jnp.dot(oa_ref[...].astype(BF16), wout_ref[0:aw, :], preferred_element_type=F32)
          + jnp.dot(os_ref[...], wout_ref[aw:aw + cw, :], preferred_element_type=F32)
          + jnp.dot(oc_ref[...], wout_ref[aw + cw:aw + 2 * cw, :], preferred_element_type=F32))
    x1_ref[...] = x1
    xn = _rms_rows(x1, g2_ref[...]).astype(BF16)
    xn_ref[...] = xn
    qp = jnp.dot(xn, wq_ref[...], preferred_element_type=F32).astype(BF16)
    for h in range(PEER_HEADS):
        qp_ref[h] = qp[:, h * LANES:(h + 1) * LANES]


def _post(x, oa, osg, oc, wout_b, g2, wq_b, tm):
    n, D = x.shape
    row = lambda w: pl.BlockSpec((tm, w), lambda i: (i, 0))
    return pl.pallas_call(
        _post_kernel,
        out_shape=(jax.ShapeDtypeStruct((n, D), F32), jax.ShapeDtypeStruct((n, D), BF16),
                   jax.ShapeDtypeStruct((PEER_HEADS, n, LANES), BF16)),
        grid=(n // tm,),
        in_specs=[row(D), row(oa.shape[1]), row(osg.shape[1]), row(oc.shape[1]),
                  _full(wout_b.shape), _full((1, D)), _full(wq_b.shape)],
        out_specs=(row(D), row(D), pl.BlockSpec((PEER_HEADS, tm, LANES), lambda i: (0, i, 0))),
        compiler_params=_cparams(("arbitrary",)),
        name="post",
    )(x, oa, osg, oc, wout_b, g2, wq_b)


def _sort_pairs(n):
    pairs = []
    p = 1
    while p < n:
        k = p
        while k >= 1:
            for j in range(k % p, n - k, 2 * k):
                for i in range(min(k, n - j - k)):
                    if (i + j) // (2 * p) == (i + j + k) // (2 * p):
                        pairs.append((i + j, i + j + k))
            k //= 2
        p *= 2
    return pairs


def _cmpx(lst, i, j):
    a, b = lst[i], lst[j]
    if b is None:
        return
    if a is None:
        lst[i], lst[j] = b, None
        return
    lst[i], lst[j] = jnp.maximum(a, b), jnp.minimum(a, b)


def _sort_desc(vals):
    n = 1
    while n < len(vals):
        n *= 2
    lst = list(vals) + [None] * (n - len(vals))
    for i, j in _sort_pairs(n):
        _cmpx(lst, i, j)
    return lst


def _top16_rows(sT):
    lst = _sort_desc([sT[SUBLANES * v:SUBLANES * (v + 1), :] for v in range(N_KEYS // SUBLANES)])
    for shift in (4, 2, 1):
        other = [pltpu.roll(x, shift, 0) for x in lst]
        lst = [jnp.maximum(lst[i], other[TOPK - 1 - i]) for i in range(TOPK)]
        d = TOPK // 2
        while d >= 1:
            for i in range(TOPK):
                if (i & d) == 0:
                    _cmpx(lst, i, i + d)
            d //= 2
    return lst


def _peer_select(q, k1, k2):
    dn = (((1,), (1,)), ((), ()))
    s1 = lax.dot_general(k1, q, dn, preferred_element_type=F32)
    s2 = lax.dot_general(k2, q, dn, preferred_element_type=F32)
    a = _top16_rows(s1)
    b = _top16_rows(s2)
    cand = {}
    for r in range(TOPK):
        for c in range(TOPK // (r + 1)):
            cand[(r, c)] = a[r] + b[c]
    tau = _sort_desc(list(cand.values()))[TOPK - 1]
    m = cand[(0, 0)]
    inf = jnp.full(tau.shape, jnp.inf, F32)
    z = jnp.zeros(tau.shape, F32)
    theta_r = []
    for r in range(TOPK):
        th = inf
        for c in range(TOPK // (r + 1)):
            sel = cand[(r, c)] >= tau
            th = jnp.where(sel, jnp.minimum(th, b[c]), th)
            z = z + jnp.where(sel, jnp.exp(cand[(r, c)] - m), 0.0)
        theta_r.append(th)
    zinv = 1.0 / z
    nv = N_KEYS // SUBLANES
    theta, e1, e2 = [], [], []
    for v in range(nv):
        rows = s1[SUBLANES * v:SUBLANES * (v + 1), :]
        th = inf
        for r in range(TOPK - 1, -1, -1):
            th = jnp.where(rows >= a[r], theta_r[r], th)
        theta.append(th)
        e1.append(jnp.exp(rows - a[0]))
        rows2 = s2[SUBLANES * v:SUBLANES * (v + 1), :]
        e2.append(jnp.exp(rows2 - b[0]) * zinv)
    return (jnp.concatenate(theta, axis=0), jnp.concatenate(e1, axis=0), s2, jnp.concatenate(e2, axis=0))


def _peer_kernel(x1_ref, xn_ref, qp_ref, keys_ref, u_ref, v_ref, o_ref,
                 th_sc, e1_sc, s2_sc, e2_sc, at_sc, ht_sc, acc_sc):
    e = pl.program_id(1)
    ne = pl.num_programs(1)
    tm = xn_ref.shape[0]
    ec = u_ref.shape[0]
    ntb = tm // LANES
    ni = ec // N_KEYS

    @pl.when(e == 0)
    def _():
        acc_sc[...] = jnp.zeros(acc_sc.shape, F32)

        def body(it, carry):
            h = it // ntb
            tb = it % ntb
            q = qp_ref[h, pl.ds(pl.multiple_of(tb * LANES, LANES), LANES), :]
            th, e1, s2, e2 = _peer_select(q, keys_ref[2 * h], keys_ref[2 * h + 1])
            th_sc[h, tb] = th
            e1_sc[h, tb] = e1
            s2_sc[h, tb] = s2
            e2_sc[h, tb] = e2
            return carry

        lax.fori_loop(0, PEER_HEADS * ntb, body, 0)

    at_sc[...] = lax.dot_general(u_ref[...], xn_ref[...], (((1,), (1,)), ((), ())),
                                 preferred_element_type=F32)

    def dense(il, carry):
        ig = e * ni + il
        r0 = pl.multiple_of(il * N_KEYS, N_KEYS)
        for tb in range(ntb):
            cols = slice(tb * LANES, (tb + 1) * LANES)
            act = jax.nn.gelu(at_sc[pl.ds(r0, N_KEYS), cols])
            gate = jnp.zeros((N_KEYS, LANES), F32)
            for h in range(PEER_HEADS):
                th = th_sc[h, tb, pl.ds(ig, 1), :]
                e1 = e1_sc[h, tb, pl.ds(ig, 1), :]
                gate = gate + jnp.where(s2_sc[h, tb] >= th, e2_sc[h, tb], 0.0) * e1
            ht_sc[pl.ds(r0, N_KEYS), cols] = (act * gate).astype(BF16)
        return carry

    lax.fori_loop(0, ni, dense, 0)
    acc_sc[...] += lax.dot_general(ht_sc[...], v_ref[...], (((0,), (0,)), ((), ())),
                                   preferred_element_type=F32)

    @pl.when(e == ne - 1)
    def _():
        o_ref[...] = x1_ref[...] + acc_sc[...]


def _peer(x1, xn, qp, keys_pad, u_b, v_b, tm, ec):
    n, D = x1.shape
    ne = u_b.shape[0] // ec
    ntb = tm // LANES
    row = lambda w: pl.BlockSpec((tm, w), lambda i, e: (i, 0))
    sel = pltpu.VMEM((PEER_HEADS, ntb, N_KEYS, LANES), F32)
    return pl.pallas_call(
        _peer_kernel,
        out_shape=jax.ShapeDtypeStruct((n, D), F32),
        grid=(n // tm, ne),
        in_specs=[row(D), row(D),
                  pl.BlockSpec((PEER_HEADS, tm, LANES), lambda i, e: (0, i, 0)),
                  pl.BlockSpec(keys_pad.shape, lambda i, e: (0, 0, 0)),
                  pl.BlockSpec((ec, D), lambda i, e: (e, 0)),
                  pl.BlockSpec((ec, D), lambda i, e: (e, 0))],
        out_specs=row(D),
        scratch_shapes=[sel, sel, sel, sel,
                        pltpu.VMEM((ec, tm), F32), pltpu.VMEM((ec, tm), BF16), pltpu.VMEM((tm, D), F32)],
        compiler_params=_cparams(("arbitrary", "arbitrary")),
        name="peer",
    )(x1, xn, qp, keys_pad, u_b, v_b)


def _final_kernel(x_ref, g_ref, o_ref):
    o_ref[...] = _rms_rows(x_ref[...], g_ref[...])


def _final_norm(x, g, tm):
    n, D = x.shape
    return pl.pallas_call(
        _final_kernel,
        out_shape=jax.ShapeDtypeStruct((n, D), F32),
        grid=(n // tm,),
        in_specs=[pl.BlockSpec((tm, D), lambda i: (i, 0)), _full((1, D))],
        out_specs=pl.BlockSpec((tm, D), lambda i: (i, 0)),
        compiler_params=_cparams(("arbitrary",)),
        name="final_norm",
    )(x, g)


def _rope_tables(pos):
    half = 32
    inv = ROPE_THETA ** (-jnp.arange(half, dtype=F32) / half)
    ang = pos.astype(F32)[:, None] * inv[None, :]
    c, s = jnp.cos(ang), jnp.sin(ang)
    cos = jnp.concatenate([c, c, c, c], axis=1)
    sin = jnp.concatenate([-s, s, -s, s], axis=1)
    return cos, sin


def _block_diag_ones(width, group):
    i = np.arange(width) // group
    return jnp.asarray((i[:, None] == i[None, :]).astype(np.float32), dtype=BF16)


def _pick_tile(n, pref):
    t = pref
    while n % t:
        t //= 2
    return t


def kernel(x_prompt, x_sample, cache_k, cache_v, state_conv, page_table, norm1_g, w_in, lam_q1, lam_k1,
           lam_q2, lam_k2, diff_norm_g, sgu_norm_g, w_s, b_s, conv_w, conv_b, conv_norm_g, w_conv_pw,
           w_out, norm2_g, peer_wq, peer_keys, peer_u, peer_v, final_norm_g):
    B, T, D = x_prompt.shape
    DB, S, _ = x_sample.shape
    depth = w_in.shape[0]
    n_phys = cache_k.shape[1]
    n_pages = page_table.shape[1]
    past = n_pages * PAGE
    aw = N_HEADS * HEAD_W
    cwid = sgu_norm_g.shape[1]
    gw = cwid // SGU_GROUPS
    n_exp = peer_u.shape[1]
    assert S == 4 and cwid == conv_norm_g.shape[1] and n_exp == N_KEYS * N_KEYS

    tm_p = _pick_tile(T, 512)
    tq = _pick_tile(T, 1024)
    tk = tq
    np_tok = B * T
    ns_tok = DB * S
    tm_post_p = _pick_tile(np_tok, 512)
    tm_post_s = _pick_tile(ns_tok, 512)
    ec = 1024
    ppb = 4

    cos_p, sin_p = _rope_tables(jnp.arange(T, dtype=jnp.int32))
    pos_s = past + jnp.repeat(jnp.arange(S, dtype=jnp.int32), DB)
    cos_s, sin_s = _rope_tables(pos_s)
    bd = _block_diag_ones(cwid, gw)

    cache_k2 = cache_k.reshape(depth * n_phys, PAGE, aw)
    cache_v2 = cache_v.reshape(depth * n_phys, PAGE, aw)
    pt_flat = page_table.reshape(-1)

    xs = jnp.transpose(x_sample, (1, 0, 2)).reshape(ns_tok, D)
    xp = x_prompt

    col = np.arange(aw)
    rowid = np.arange(32)
    qmask = jnp.asarray((col[None, :] // 64 == rowid[:, None] // 4).astype(np.float32))

    kp_l, vp_l, ks_l, vs_l, cp_l, cs_l, gs_l = [], [], [], [], [], [], []
    for l in range(depth):
        lam_init = 0.8 - 0.6 * math.exp(-0.3 * l)
        win_b = w_in[l].astype(BF16)
        wout_b = w_out[l].astype(BF16)
        wq_b = peer_wq[l].astype(BF16)
        wpw_b = w_conv_pw[l].astype(BF16)
        u_b = peer_u[l].astype(BF16)
        v_b = peer_v[l].astype(BF16)
        hw = peer_keys.shape[-1]
        kz = jnp.zeros(peer_keys.shape[1:4] + (LANES - hw,), F32)
        keys_pad = jnp.stack([jnp.concatenate([peer_keys[l, :, 0], kz[:, 0]], axis=-1),
                              jnp.concatenate([kz[:, 1], peer_keys[l, :, 1]], axis=-1)], axis=1)
        keys_pad = keys_pad.reshape(2 * PEER_HEADS, N_KEYS, LANES).astype(BF16)
        lam4 = jnp.stack([lam_q1[l], lam_k1[l], lam_q2[l], lam_k2[l]])
        g1 = norm1_g[l][None, :]
        g2 = norm2_g[l][None, :]
        gd = diff_norm_g[l][None, :]
        sgug = sgu_norm_g[l][None, :]
        cb = conv_b[l][None, :]
        cg = conv_norm_g[l][None, :]
        bs_full = jnp.repeat(b_s[l].T, gw, axis=1)
        wsx = jnp.repeat(jnp.transpose(w_s[l][:, :S, :S], (1, 2, 0)), gw, axis=2)
        bsx = bs_full[:S]

        qa, qb, kf, kb, vf, vb, osg, oc, crow = _proj_prompt(
            xp, g1, win_b, cos_p, sin_p, sgug, w_s[l], bs_full, conv_w[l], cb, cg, wpw_b, bd, tm_p)
        oa = _attn_prompt(qa, qb, kb, vb, lam4, gd, lam_init, tq, tk)
        x1, xn, qp = _post(xp.reshape(np_tok, D), oa.reshape(np_tok, aw), osg.reshape(np_tok, cwid),
                           oc.reshape(np_tok, cwid), wout_b, g2, wq_b, tm_post_p)
        xp = _peer(x1, xn, qp, keys_pad, u_b, v_b, tm_post_p, ec).reshape(B, T, D)
        kp_l.append(kf.reshape(B, T, N_HEADS, HEAD_W))
        vp_l.append(vf.reshape(B, T, N_HEADS, HEAD_W))
        cp_l.append(crow)

        prefix = jnp.transpose(state_conv[l], (1, 0, 2))
        qs, ksf, vsf, osg_s, oc_s, h_s, vn_s = _proj_sample(
            xs, g1, win_b, cos_s, sin_s, sgug, wsx, bsx, prefix, conv_w[l], cb, cg, wpw_b, bd)
        to_b = lambda a: jnp.transpose(a.reshape(S, DB, a.shape[-1]), (1, 0, 2))
        q_b, k_b, v_b2 = to_b(qs), to_b(ksf), to_b(vsf)
        q32 = jnp.tile(q_b, (1, 8, 1)) * qmask[None]
        pad8 = lambda a: jnp.concatenate([a, jnp.zeros_like(a)], axis=1)
        oa_s8 = _attn_sample(pt_flat, q32, pad8(k_b), pad8(v_b2), cache_k2, cache_v2, lam4, gd, lam_init,
                             l * n_phys, n_pages, ppb)
        oa_s = jnp.transpose(oa_s8[:, :S], (1, 0, 2)).reshape(ns_tok, aw)
        x1s, xns, qps = _post(xs, oa_s, osg_s, oc_s, wout_b, g2, wq_b, tm_post_s)
        xs = _peer(x1s, xns, qps, keys_pad, u_b, v_b, tm_post_s, ec)
        ks_l.append(k_b.reshape(DB, S, N_HEADS, HEAD_W))
        vs_l.append(v_b2.reshape(DB, S, N_HEADS, HEAD_W))
        cs_l.append(jnp.concatenate([state_conv[l][:, S:], to_b(h_s)], axis=1))
        gs_l.append(to_b(vn_s))

    fg = final_norm_g[None, :]
    y_prompt = _final_norm(xp.reshape(np_tok, D), fg, tm_post_p).reshape(B, T, D)
    y_sample = jnp.transpose(_final_norm(xs, fg, tm_post_s).reshape(S, DB, D), (1, 0, 2))
    return (y_prompt, y_sample, jnp.stack(kp_l), jnp.stack(vp_l), jnp.stack(ks_l), jnp.stack(vs_l),
            jnp.stack(cp_l), jnp.stack(cs_l), jnp.stack(gs_l))
```

```python
import functools
import math

import numpy as np
import jax
import jax.numpy as jnp
from jax import lax
from jax.experimental import pallas as pl
from jax.experimental.pallas import tpu as pltpu

F32 = jnp.float32
BF16 = jnp.bfloat16

LANES = 128
SUBLANES = 8
EPS = 1e-6
NEG_INF = -1e30
ROPE_THETA = 10000.0

N_HEADS = 4
HEAD_W = 128
SGU_GROUPS = 4
CONV_GROUPS = 4
CONV_K = 31
CHUNK = 128
PAGE = 128
PEER_HEADS = 8
N_KEYS = 128
TOPK = 16

VMEM_LIMIT = 56 * 1024 * 1024

TM_PROJ = 512
TQ_ATTN = 1024
TM_TOKENS = 512
EXPERT_BLOCK = 2048


def _cparams(sem):
    return pltpu.CompilerParams(dimension_semantics=sem, vmem_limit_bytes=VMEM_LIMIT)


def _full(shape):
    n = len(shape)
    return pl.BlockSpec(shape, lambda *_: (0,) * n)


def _sigmoid(x):
    return 1.0 / (1.0 + jnp.exp(-x))


def _rms_rows(x, g):
    return x * lax.rsqrt(jnp.mean(x * x, axis=-1, keepdims=True) + EPS) * g


def _group_mean_sq(x, bd, group):
    x2 = x * x
    hi = x2.astype(BF16)
    lo = (x2 - hi.astype(F32)).astype(BF16)
    s = jnp.dot(hi, bd, preferred_element_type=F32) + jnp.dot(lo, bd, preferred_element_type=F32)
    return s * (1.0 / group)


def _rope_block(xh, cos, sin, lo_half):
    sw = jnp.where(lo_half, pltpu.roll(xh, LANES - 32, 1), pltpu.roll(xh, 32, 1))
    return xh * cos + sw * sin


def _lam_from(lam4):
    a = jnp.sum(lam4[0:1, :] * lam4[1:2, :], axis=-1, keepdims=True)
    b = jnp.sum(lam4[2:3, :] * lam4[3:4, :], axis=-1, keepdims=True)
    return jnp.exp(a) - jnp.exp(b)


def _proj_p_kernel(x_ref, g1_ref, win_ref, cos_ref, sin_ref, sgug_ref, ws_ref, bs_ref,
                   cw_ref, cb_ref, cg_ref, wpw_ref, bd_ref,
                   qa_ref, qb_ref, kf_ref, kb_ref, vf_ref, vb_ref, osgu_ref, oconv_ref, crow_ref,
                   hbuf):
    t = pl.program_id(1)
    tm = x_ref.shape[0]
    x = x_ref[...]
    hn = _rms_rows(x, g1_ref[...]).astype(BF16)
    p = jnp.dot(hn, win_ref[...], preferred_element_type=F32)
    aw = N_HEADS * HEAD_W
    q = p[:, 0:aw]
    k = p[:, aw:2 * aw]
    v = p[:, 2 * aw:3 * aw]
    o = 3 * aw
    cw = osgu_ref.shape[1]
    su = p[:, o:o + cw]
    sv = p[:, o + cw:o + 2 * cw]
    ca = p[:, o + 2 * cw:o + 3 * cw]
    cg = p[:, o + 3 * cw:o + 4 * cw]

    cos = cos_ref[...]
    sin = sin_ref[...]
    lane = lax.broadcasted_iota(jnp.int32, (tm, LANES), 1)
    lo_half = (lane & 32) == 0
    first = (lane & 64) == 0
    for h in range(N_HEADS):
        sl = slice(h * HEAD_W, (h + 1) * HEAD_W)
        qr = _rope_block(q[:, sl], cos, sin, lo_half) * 0.125
        qa_ref[:, sl] = jnp.where(first, qr, 0.0).astype(BF16)
        qb_ref[:, sl] = jnp.where(first, 0.0, qr).astype(BF16)
        kr = _rope_block(k[:, sl], cos, sin, lo_half)
        kf_ref[:, sl] = kr
        kb_ref[:, sl] = kr.astype(BF16)
    vf_ref[...] = v
    vb_ref[...] = v.astype(BF16)

    bd = bd_ref[...]
    u = jax.nn.gelu(su)
    vg = jax.nn.gelu(sv)
    vn = vg * lax.rsqrt(_group_mean_sq(vg, bd, cw // SGU_GROUPS) + EPS) * sgug_ref[...]
    vnb = vn.astype(BF16)
    r_i = lax.broadcasted_iota(jnp.int32, (CHUNK, CHUNK), 0)
    c_i = lax.broadcasted_iota(jnp.int32, (CHUNK, CHUNK), 1)
    tril = r_i >= c_i
    ws = [jnp.where(tril, ws_ref[g], 0.0).astype(BF16) for g in range(SGU_GROUPS)]
    left = c_i < 64
    bias = bs_ref[...]
    for c in range(tm // CHUNK):
        rows = slice(c * CHUNK, (c + 1) * CHUNK)
        parts = []
        for gp in range(SGU_GROUPS // 2):
            r = vnb[rows, gp * LANES:(gp + 1) * LANES]
            parts.append(jnp.where(left,
                                   jnp.dot(ws[2 * gp], r, preferred_element_type=F32),
                                   jnp.dot(ws[2 * gp + 1], r, preferred_element_type=F32)))
        mixed = jnp.concatenate(parts, axis=1) + bias
        osgu_ref[rows, :] = (u[rows, :] * mixed).astype(BF16)

    hg = ca * _sigmoid(cg)

    @pl.when(t == 0)
    def _():
        hbuf[0:32, :] = jnp.zeros((32, cw), F32)

    hbuf[32:32 + tm, :] = hg
    y = jnp.zeros((tm, cw), F32) + cb_ref[...]
    for j in range(CONV_K):
        y = y + hbuf[pl.ds(2 + j, tm), :] * cw_ref[j:j + 1, :]
    yn = y * lax.rsqrt(_group_mean_sq(y, bd, cw // CONV_GROUPS) + EPS) * cg_ref[...]
    ys = yn * _sigmoid(yn)
    oconv_ref[...] = jnp.dot(ys.astype(BF16), wpw_ref[...], preferred_element_type=F32).astype(BF16)
    crow_ref[...] = hbuf[pl.ds(tm + 2, CONV_K - 1), :]
    hbuf[0:32, :] = hbuf[tm:tm + 32, :]


def _proj_prompt(x, g1, win_b, cos, sin, sgug, ws, bs_full, cw, cb, cg, wpw_b, bd, tm):
    B, T, D = x.shape
    PW = win_b.shape[1]
    aw = N_HEADS * HEAD_W
    cwid = sgug.shape[1]
    row = lambda w: pl.BlockSpec((None, tm, w), lambda b, t: (b, t, 0))
    outs = (
        jax.ShapeDtypeStruct((B, T, aw), BF16), jax.ShapeDtypeStruct((B, T, aw), BF16),
        jax.ShapeDtypeStruct((B, T, aw), F32), jax.ShapeDtypeStruct((B, T, aw), BF16),
        jax.ShapeDtypeStruct((B, T, aw), F32), jax.ShapeDtypeStruct((B, T, aw), BF16),
        jax.ShapeDtypeStruct((B, T, cwid), BF16), jax.ShapeDtypeStruct((B, T, cwid), BF16),
        jax.ShapeDtypeStruct((B, CONV_K - 1, cwid), F32),
    )
    return pl.pallas_call(
        _proj_p_kernel,
        out_shape=outs,
        grid=(B, T // tm),
        in_specs=[row(D), _full((1, D)), _full((D, PW)),
                  pl.BlockSpec((tm, LANES), lambda b, t: (t, 0)),
                  pl.BlockSpec((tm, LANES), lambda b, t: (t, 0)),
                  _full((1, cwid)), _full(ws.shape), _full(bs_full.shape),
                  _full(cw.shape), _full((1, cwid)), _full((1, cwid)), _full(wpw_b.shape), _full(bd.shape)],
        out_specs=(row(aw), row(aw), row(aw), row(aw), row(aw), row(aw), row(cwid), row(cwid),
                   pl.BlockSpec((None, CONV_K - 1, cwid), lambda b, t: (b, 0, 0))),
        scratch_shapes=[pltpu.VMEM((tm + 32, cwid), F32)],
        compiler_params=_cparams(("arbitrary", "arbitrary")),
        name="proj_prompt",
    )(x, g1, win_b, cos, sin, sgug, ws, bs_full, cw, cb, cg, wpw_b, bd)


def _proj_s_kernel(x_ref, g1_ref, win_ref, cos_ref, sin_ref, sgug_ref, wsx_ref, bsx_ref,
                   pre_ref, cw_ref, cb_ref, cg_ref, wpw_ref, bd_ref,
                   q_ref, kf_ref, vf_ref, osgu_ref, oconv_ref, h_ref, vn_ref):
    n = x_ref.shape[0]
    nb = pre_ref.shape[1]
    S = n // nb
    x = x_ref[...]
    hn = _rms_rows(x, g1_ref[...]).astype(BF16)
    p = jnp.dot(hn, win_ref[...], preferred_element_type=F32)
    aw = N_HEADS * HEAD_W
    q = p[:, 0:aw]
    k = p[:, aw:2 * aw]
    o = 3 * aw
    cw = osgu_ref.shape[1]
    su = p[:, o:o + cw]
    sv = p[:, o + cw:o + 2 * cw]
    ca = p[:, o + 2 * cw:o + 3 * cw]
    cg = p[:, o + 3 * cw:o + 4 * cw]
    cos = cos_ref[...]
    sin = sin_ref[...]
    lane = lax.broadcasted_iota(jnp.int32, (n, LANES), 1)
    lo_half = (lane & 32) == 0
    for h in range(N_HEADS):
        sl = slice(h * HEAD_W, (h + 1) * HEAD_W)
        q_ref[:, sl] = _rope_block(q[:, sl], cos, sin, lo_half) * 0.125
        kf_ref[:, sl] = _rope_block(k[:, sl], cos, sin, lo_half)
    vf_ref[...] = p[:, 2 * aw:3 * aw]

    bd = bd_ref[...]
    u = jax.nn.gelu(su)
    vg = jax.nn.gelu(sv)
    vn = vg * lax.rsqrt(_group_mean_sq(vg, bd, cw // SGU_GROUPS) + EPS) * sgug_ref[...]
    vn_ref[...] = vn
    for t in range(S):
        mixed = jnp.zeros((nb, cw), F32) + bsx_ref[t:t + 1, :]
        for s in range(t + 1):
            mixed = mixed + wsx_ref[t, s:s + 1, :] * vn[s * nb:(s + 1) * nb, :]
        osgu_ref[t * nb:(t + 1) * nb, :] = (u[t * nb:(t + 1) * nb, :] * mixed).astype(BF16)

    hg = ca * _sigmoid(cg)
    h_ref[...] = hg
    npre = CONV_K - 1
    ys = []
    for t in range(S):
        y = jnp.zeros((nb, cw), F32) + cb_ref[...]
        for j in range(CONV_K):
            kk = t + j
            src = pre_ref[kk] if kk < npre else hg[(kk - npre) * nb:(kk - npre + 1) * nb, :]
            y = y + src * cw_ref[j:j + 1, :]
        ys.append(y)
    y = jnp.concatenate(ys, axis=0)
    yn = y * lax.rsqrt(_group_mean_sq(y, bd, cw // CONV_GROUPS) + EPS) * cg_ref[...]
    ysw = yn * _sigmoid(yn)
    oconv_ref[...] = jnp.dot(ysw.astype(BF16), wpw_ref[...], preferred_element_type=F32).astype(BF16)


def _proj_sample(x, g1, win_b, cos, sin, sgug, wsx, bsx, prefix, cw, cb, cg, wpw_b, bd):
    n, D = x.shape
    aw = N_HEADS * HEAD_W
    cwid = sgug.shape[1]
    args = (x, g1, win_b, cos, sin, sgug, wsx, bsx, prefix, cw, cb, cg, wpw_b, bd)
    outs = (jax.ShapeDtypeStruct((n, aw), F32), jax.ShapeDtypeStruct((n, aw), F32),
            jax.ShapeDtypeStruct((n, aw), F32),
            jax.ShapeDtypeStruct((n, cwid), BF16), jax.ShapeDtypeStruct((n, cwid), BF16),
            jax.ShapeDtypeStruct((n, cwid), F32), jax.ShapeDtypeStruct((n, cwid), F32))
    return pl.pallas_call(
        _proj_s_kernel,
        out_shape=outs,
        grid=(1,),
        in_specs=[_full(a.shape) for a in args],
        out_specs=tuple(_full(o.shape) for o in outs),
        compiler_params=_cparams(("arbitrary",)),
        name="proj_sample",
    )(*args)


def _attn_p_kernel(qt_ref, kt_ref, lam4_ref, g_ref, qa_ref, qb_ref, k_ref, v_ref, o_ref,
                   m0, l0, a0, m1, l1, a1, *, lam_init):
    s_id = pl.program_id(2)
    qi = qt_ref[s_id]
    ki = kt_ref[s_id]
    tq = qa_ref.shape[0]
    tk = k_ref.shape[0]

    @pl.when(ki == 0)
    def _():
        for m, l, a in ((m0, l0, a0), (m1, l1, a1)):
            m[...] = jnp.full(m.shape, NEG_INF, F32)
            l[...] = jnp.zeros(l.shape, F32)
            a[...] = jnp.zeros(a.shape, F32)

    kk = k_ref[...]
    vv = v_ref[...]
    qpos = qi * tq + lax.broadcasted_iota(jnp.int32, (tq, tk), 0)
    kpos = ki * tk + lax.broadcasted_iota(jnp.int32, (tq, tk), 1)
    keep = kpos <= qpos
    for q_ref, m, l, a in ((qa_ref, m0, l0, a0), (qb_ref, m1, l1, a1)):
        s = lax.dot_general(q_ref[...], kk, (((1,), (1,)), ((), ())), preferred_element_type=F32)
        s = jnp.where(keep, s, NEG_INF)
        m_old = m[...]
        m_new = jnp.maximum(m_old, jnp.max(s, axis=-1, keepdims=True))
        alpha = jnp.exp(m_old - m_new)
        p = jnp.exp(s - m_new)
        l[...] = alpha * l[...] + jnp.sum(p, axis=-1, keepdims=True)
        a[...] = alpha * a[...] + jnp.dot(p.astype(BF16), vv, preferred_element_type=F32)
        m[...] = m_new

    @pl.when(ki == (qi * tq + tq - 1) // tk)
    def _():
        lam = _lam_from(lam4_ref[...]) + lam_init
        o = a0[...] / l0[...] - lam * (a1[...] / l1[...])
        o_ref[...] = (_rms_rows(o, g_ref[...]) * (1.0 - lam_init)).astype(o_ref.dtype)


def _attn_prompt(qa, qb, kb, vb, lam4, g, lam_init, tq, tk):
    B, T, aw = qa.shape
    nq = T // tq
    qs, ks = [], []
    for i in range(nq):
        last = (i * tq + tq - 1) // tk
        for j in range(last + 1):
            qs.append(i)
            ks.append(j)
    qt = jnp.asarray(np.array(qs, np.int32))
    kt = jnp.asarray(np.array(ks, np.int32))
    qspec = pl.BlockSpec((None, tq, HEAD_W), lambda b, h, s, qt, kt: (b, qt[s], h))
    kspec = pl.BlockSpec((None, tk, HEAD_W), lambda b, h, s, qt, kt: (b, kt[s], h))
    grid_spec = pltpu.PrefetchScalarGridSpec(
        num_scalar_prefetch=2,
        grid=(B, N_HEADS, len(qs)),
        in_specs=[pl.BlockSpec(lam4.shape, lambda b, h, s, qt, kt: (0, 0)),
                  pl.BlockSpec(g.shape, lambda b, h, s, qt, kt: (0, 0)),
                  qspec, qspec, kspec, kspec],
        out_specs=qspec,
        scratch_shapes=[pltpu.VMEM((tq, 1), F32), pltpu.VMEM((tq, 1), F32), pltpu.VMEM((tq, HEAD_W), F32),
                        pltpu.VMEM((tq, 1), F32), pltpu.VMEM((tq, 1), F32), pltpu.VMEM((tq, HEAD_W), F32)],
    )
    return pl.pallas_call(
        functools.partial(_attn_p_kernel, lam_init=lam_init),
        out_shape=jax.ShapeDtypeStruct((B, T, aw), BF16),
        grid_spec=grid_spec,
        compiler_params=_cparams(("arbitrary", "arbitrary", "arbitrary")),
        name="attn_prompt",
    )(qt, kt, lam4, g, qa, qb, kb, vb)


def _attn_s_kernel(pt_ref, lam4_ref, g_ref, q_ref, knx_ref, vnx_ref, *rest, lam_init, ppb, n_new):
    k_refs = rest[:ppb]
    v_refs = rest[ppb:2 * ppb]
    o_ref = rest[2 * ppb]
    m_sc, l_sc, a_sc = rest[2 * ppb + 1:]
    j = pl.program_id(1)
    nj = pl.num_programs(1)

    @pl.when(j == 0)
    def _():
        m_sc[...] = jnp.full(m_sc.shape, NEG_INF, F32)
        l_sc[...] = jnp.zeros(l_sc.shape, F32)
        a_sc[...] = jnp.zeros(a_sc.shape, F32)

    q = q_ref[...]
    qb = q.astype(BF16)
    rows = q.shape[0]
    ncol = k_refs[0].shape[0]
    col = lax.broadcasted_iota(jnp.int32, (rows, ncol), 1)
    row = lax.broadcasted_iota(jnp.int32, (rows, ncol), 0)
    own = (col & (N_HEADS - 1)) == (row >> 3)
    ss = []
    for kr in k_refs:
        kk = kr[...].astype(BF16)
        s = lax.dot_general(qb, kk, (((1,), (1,)), ((), ())), preferred_element_type=F32)
        ss.append(jnp.where(own, s, NEG_INF))
    m_old = m_sc[...]
    m_new = m_old
    for s in ss:
        m_new = jnp.maximum(m_new, jnp.max(s, axis=-1, keepdims=True))
    alpha = jnp.exp(m_old - m_new)
    l = alpha * l_sc[...]
    acc = alpha * a_sc[...]
    for s, vr in zip(ss, v_refs):
        p = jnp.exp(s - m_new)
        l = l + jnp.sum(p, axis=-1, keepdims=True)
        acc = acc + jnp.dot(p.astype(BF16), vr[...].astype(BF16), preferred_element_type=F32)
    l_sc[...] = l
    a_sc[...] = acc
    m_sc[...] = m_new

    @pl.when(j == nj - 1)
    def _():
        tok = lax.broadcasted_iota(jnp.int32, (rows, 1), 0) & (n_new - 1)
        sn = []
        for c in range(n_new):
            sc = jnp.sum(q * knx_ref[c], axis=-1, keepdims=True)
            sn.append(jnp.where(tok >= c, sc, NEG_INF))
        m_old = m_sc[...]
        m_new = m_old
        for sc in sn:
            m_new = jnp.maximum(m_new, sc)
        alpha = jnp.exp(m_old - m_new)
        l = alpha * l_sc[...]
        acc = alpha * a_sc[...]
        for c, sc in enumerate(sn):
            pc = jnp.exp(sc - m_new)
            l = l + pc
            acc = acc + pc * vnx_ref[c]
        xo = acc / l
        lam = _lam_from(lam4_ref[...]) + lam_init
        o = xo - lam * pltpu.roll(xo, rows - n_new, 0)
        o_ref[...] = _rms_rows(o, g_ref[...]) * (1.0 - lam_init)


def _attn_sample(pt_flat, q32, knx, vnx, cache_k2, cache_v2, lam4, g, lam_init, layer_base, n_pages, ppb):
    nb = q32.shape[0]
    n_new = knx.shape[1]
    steps = n_pages // ppb
    prow = PAGE * N_HEADS

    def page_spec(p):
        return pl.BlockSpec((prow, HEAD_W),
                            lambda b, j, pt, p=p: (layer_base + pt[b * n_pages + j * ppb + p], 0))

    grid_spec = pltpu.PrefetchScalarGridSpec(
        num_scalar_prefetch=1,
        grid=(nb, steps),
        in_specs=[pl.BlockSpec(lam4.shape, lambda b, j, pt: (0, 0)),
                  pl.BlockSpec(g.shape, lambda b, j, pt: (0, 0)),
                  pl.BlockSpec((None, 32, HEAD_W), lambda b, j, pt: (b, 0, 0)),
                  pl.BlockSpec((None, n_new, 32, HEAD_W), lambda b, j, pt: (b, 0, 0, 0)),
                  pl.BlockSpec((None, n_new, 32, HEAD_W), lambda b, j, pt: (b, 0, 0, 0))]
                 + [page_spec(p) for p in range(ppb)] + [page_spec(p) for p in range(ppb)],
        out_specs=pl.BlockSpec((None, 32, HEAD_W), lambda b, j, pt: (b, 0, 0)),
        scratch_shapes=[pltpu.VMEM((32, 1), F32), pltpu.VMEM((32, 1), F32), pltpu.VMEM((32, HEAD_W), F32)],
    )
    return pl.pallas_call(
        functools.partial(_attn_s_kernel, lam_init=lam_init, ppb=ppb, n_new=n_new),
        out_shape=jax.ShapeDtypeStruct((nb, 32, HEAD_W), F32),
        grid_spec=grid_spec,
        compiler_params=_cparams(("arbitrary", "arbitrary")),
        name="attn_sample",
    )(pt_flat, lam4, g, q32, knx, vnx, *([cache_k2] * ppb), *([cache_v2] * ppb))


def _post_kernel(x_ref, oa_ref, os_ref, oc_ref, wout_ref, g2_ref, wq_ref, x1_ref, xn_ref, qp_ref):
    aw = oa_ref.shape[1]
    cw = os_ref.shape[1]
    x1 = (x_ref[...]
          + jnp.dot(oa_ref[...].astype(BF16), wout_ref[0:aw, :], preferred_element_type=F32)
          + jnp.dot(os_ref[...], wout_ref[aw:aw + cw, :], preferred_element_type=F32)
          + jnp.dot(oc_ref[...], wout_ref[aw + cw:aw + 2 * cw, :], preferred_element_type=F32))
    x1_ref[...] = x1
    xn = _rms_rows(x1, g2_ref[...]).astype(BF16)
    xn_ref[...] = xn
    qp = jnp.dot(xn, wq_ref[...], preferred_element_type=F32).astype(BF16)
    for h in range(PEER_HEADS):
        qp_ref[h] = qp[:, h * LANES:(h + 1) * LANES]


def _post(x, oa, osg, oc, wout_b, g2, wq_b, tm):
    n, D = x.shape
    row = lambda w: pl.BlockSpec((tm, w), lambda i: (i, 0))
    return pl.pallas_call(
        _post_kernel,
        out_shape=(jax.ShapeDtypeStruct((n, D), F32), jax.ShapeDtypeStruct((n, D), BF16),
                   jax.ShapeDtypeStruct((PEER_HEADS, n, LANES), BF16)),
        grid=(n // tm,),
        in_specs=[row(D), row(oa.shape[1]), row(osg.shape[1]), row(oc.shape[1]),
                  _full(wout_b.shape), _full((1, D)), _full(wq_b.shape)],
        out_specs=(row(D), row(D), pl.BlockSpec((PEER_HEADS, tm, LANES), lambda i: (0, i, 0))),
        compiler_params=_cparams(("arbitrary",)),
        name="post",
    )(x, oa, osg, oc, wout_b, g2, wq_b)


def _sort_pairs(n):
    pairs = []
    p = 1
    while p < n:
        k = p
        while k >= 1:
            for j in range(k % p, n - k, 2 * k):
                for i in range(min(k, n - j - k)):
                    if (i + j) // (2 * p) == (i + j + k) // (2 * p):
                        pairs.append((i + j, i + j + k))
            k //= 2
        p *= 2
    return pairs


def _cmpx(lst, i, j):
    a, b = lst[i], lst[j]
    if b is None:
        return
    if a is None:
        lst[i], lst[j] = b, None
        return
    lst[i], lst[j] = jnp.maximum(a, b), jnp.minimum(a, b)


def _sort_desc(vals):
    n = 1
    while n < len(vals):
        n *= 2
    lst = list(vals) + [None] * (n - len(vals))
    for i, j in _sort_pairs(n):
        _cmpx(lst, i, j)
    return lst


def _top16_rows(sT):
    lst = _sort_desc([sT[SUBLANES * v:SUBLANES * (v + 1), :] for v in range(N_KEYS // SUBLANES)])
    for shift in (4, 2, 1):
        other = [pltpu.roll(x, shift, 0) for x in lst]
        lst = [jnp.maximum(lst[i], other[TOPK - 1 - i]) for i in range(TOPK)]
        d = TOPK // 2
        while d >= 1:
            for i in range(TOPK):
                if (i & d) == 0:
                    _cmpx(lst, i, i + d)
            d //= 2
    return lst


def _peer_select(q, k1, k2):
    dn = (((1,), (1,)), ((), ()))
    s1 = lax.dot_general(k1, q, dn, preferred_element_type=F32)
    s2 = lax.dot_general(k2, q, dn, preferred_element_type=F32)
    a = _top16_rows(s1)
    b = _top16_rows(s2)
    cand = {}
    for r in range(TOPK):
        for c in range(TOPK // (r + 1)):
            cand[(r, c)] = a[r] + b[c]
    tau = _sort_desc(list(cand.values()))[TOPK - 1]
    m = cand[(0, 0)]
    inf = jnp.full(tau.shape, jnp.inf, F32)
    z = jnp.zeros(tau.shape, F32)
    theta_r = []
    for r in range(TOPK):
        th = inf
        for c in range(TOPK // (r + 1)):
            sel = cand[(r, c)] >= tau
            th = jnp.where(sel, jnp.minimum(th, b[c]), th)
            z = z + jnp.where(sel, jnp.exp(cand[(r, c)] - m), 0.0)
        theta_r.append(th)
    zinv = 1.0 / z
    nv = N_KEYS // SUBLANES
    theta, e1, e2 = [], [], []
    for v in range(nv):
        rows = s1[SUBLANES * v:SUBLANES * (v + 1), :]
        th = inf
        for r in range(TOPK - 1, -1, -1):
            th = jnp.where(rows >= a[r], theta_r[r], th)
        theta.append(th)
        e1.append(jnp.exp(rows - a[0]))
        rows2 = s2[SUBLANES * v:SUBLANES * (v + 1), :]
        e2.append(jnp.exp(rows2 - b[0]) * zinv)
    return (jnp.concatenate(theta, axis=0), jnp.concatenate(e1, axis=0), s2, jnp.concatenate(e2, axis=0))


def _peer_kernel(x1_ref, xn_ref, qp_ref, keys_ref, u_ref, v_ref, o_ref,
                 th_sc, e1_sc, s2_sc, e2_sc, at_sc, ht_sc, acc_sc):
    e = pl.program_id(1)
    ne = pl.num_programs(1)
    tm = xn_ref.shape[0]
    ec = u_ref.shape[0]
    ntb = tm // LANES
    ni = ec // N_KEYS

    @pl.when(e == 0)
    def _():
        acc_sc[...] = jnp.zeros(acc_sc.shape, F32)

        def body(it, carry):
            h = it // ntb
            tb = it % ntb
            q = qp_ref[h, pl.ds(pl.multiple_of(tb * LANES, LANES), LANES), :]
            th, e1, s2, e2 = _peer_select(q, keys_ref[2 * h], keys_ref[2 * h + 1])
            th_sc[h, tb] = th
            e1_sc[h, tb] = e1
            s2_sc[h, tb] = s2
            e2_sc[h, tb] = e2
            return carry

        lax.fori_loop(0, PEER_HEADS * ntb, body, 0)

    at_sc[...] = lax.dot_general(u_ref[...], xn_ref[...], (((1,), (1,)), ((), ())),
                                 preferred_element_type=F32)

    def dense(il, carry):
        ig = e * ni + il
        r0 = pl.multiple_of(il * N_KEYS, N_KEYS)
        for tb in range(ntb):
            cols = slice(tb * LANES, (tb + 1) * LANES)
            act = jax.nn.gelu(at_sc[pl.ds(r0, N_KEYS), cols])
            gate = jnp.zeros((N_KEYS, LANES), F32)
            for h in range(PEER_HEADS):
                th = th_sc[h, tb, pl.ds(ig, 1), :]
                e1 = e1_sc[h, tb, pl.ds(ig, 1), :]
                gate = gate + jnp.where(s2_sc[h, tb] >= th, e2_sc[h, tb], 0.0) * e1
            ht_sc[pl.ds(r0, N_KEYS), cols] = (act * gate).astype(BF16)
        return carry

    lax.fori_loop(0, ni, dense, 0)
    acc_sc[...] += lax.dot_general(ht_sc[...], v_ref[...], (((0,), (0,)), ((), ())),
                                   preferred_element_type=F32)

    @pl.when(e == ne - 1)
    def _():
        o_ref[...] = x1_ref[...] + acc_sc[...]


def _peer(x1, xn, qp, keys_pad, u_b, v_b, tm, ec):
    n, D = x1.shape
    ne = u_b.shape[0] // ec
    ntb = tm // LANES
    row = lambda w: pl.BlockSpec((tm, w), lambda i, e: (i, 0))
    sel = pltpu.VMEM((PEER_HEADS, ntb, N_KEYS, LANES), F32)
    return pl.pallas_call(
        _peer_kernel,
        out_shape=jax.ShapeDtypeStruct((n, D), F32),
        grid=(n // tm, ne),
        in_specs=[row(D), row(D),
                  pl.BlockSpec((PEER_HEADS, tm, LANES), lambda i, e: (0, i, 0)),
                  pl.BlockSpec(keys_pad.shape, lambda i, e: (0, 0, 0)),
                  pl.BlockSpec((ec, D), lambda i, e: (e, 0)),
                  pl.BlockSpec((ec, D), lambda i, e: (e, 0))],
        out_specs=row(D),
        scratch_shapes=[sel, sel, sel, sel,
                        pltpu.VMEM((ec, tm), F32), pltpu.VMEM((ec, tm), BF16), pltpu.VMEM((tm, D), F32)],
        compiler_params=_cparams(("arbitrary", "arbitrary")),
        name="peer",
    )(x1, xn, qp, keys_pad, u_b, v_b)


def _final_kernel(x_ref, g_ref, o_ref):
    o_ref[...] = _rms_rows(x_ref[...], g_ref[...])


def _final_norm(x, g, tm):
    n, D = x.shape
    return pl.pallas_call(
        _final_kernel,
        out_shape=jax.ShapeDtypeStruct((n, D), F32),
        grid=(n // tm,),
        in_specs=[pl.BlockSpec((tm, D), lambda i: (i, 0)), _full((1, D))],
        out_specs=pl.BlockSpec((tm, D), lambda i: (i, 0)),
        compiler_params=_cparams(("arbitrary",)),
        name="final_norm",
    )(x, g)


def _rope_tables(pos):
    half = 32
    inv = ROPE_THETA ** (-jnp.arange(half, dtype=F32) / half)
    ang = pos.astype(F32)[:, None] * inv[None, :]
    c, s = jnp.cos(ang), jnp.sin(ang)
    cos = jnp.concatenate([c, c, c, c], axis=1)
    sin = jnp.concatenate([-s, s, -s, s], axis=1)
    return cos, sin


def _block_diag_ones(width, group):
    i = np.arange(width) // group
    return jnp.asarray((i[:, None] == i[None, :]).astype(np.float32), dtype=BF16)


def _pick_tile(n, pref):
    t = pref
    while n % t:
        t //= 2
    return t


def kernel(x_prompt, x_sample, cache_k, cache_v, state_conv, page_table, norm1_g, w_in, lam_q1, lam_k1,
           lam_q2, lam_k2, diff_norm_g, sgu_norm_g, w_s, b_s, conv_w, conv_b, conv_norm_g, w_conv_pw,
           w_out, norm2_g, peer_wq, peer_keys, peer_u, peer_v, final_norm_g):
    B, T, D = x_prompt.shape
    DB, S, _ = x_sample.shape
    depth = w_in.shape[0]
    n_phys = cache_k.shape[1]
    n_pages = page_table.shape[1]
    past = n_pages * PAGE
    aw = N_HEADS * HEAD_W
    cwid = sgu_norm_g.shape[1]
    gw = cwid // SGU_GROUPS
    n_exp = peer_u.shape[1]
    assert S == 4 and cwid == conv_norm_g.shape[1] and n_exp == N_KEYS * N_KEYS

    tm_p = _pick_tile(T, TM_PROJ)
    tq = _pick_tile(T, TQ_ATTN)
    tk = tq
    np_tok = B * T
    ns_tok = DB * S
    tm_post_p = _pick_tile(np_tok, TM_TOKENS)
    tm_post_s = _pick_tile(ns_tok, TM_TOKENS)
    ec = EXPERT_BLOCK
    ppb = next(p for p in (16, 8, 4, 2, 1) if n_pages % p == 0)

    cos_p, sin_p = _rope_tables(jnp.arange(T, dtype=jnp.int32))
    pos_s = past + jnp.repeat(jnp.arange(S, dtype=jnp.int32), DB)
    cos_s, sin_s = _rope_tables(pos_s)
    bd = _block_diag_ones(cwid, gw)

    cache_k2 = cache_k.reshape(depth * n_phys * PAGE * N_HEADS, HEAD_W)
    cache_v2 = cache_v.reshape(depth * n_phys * PAGE * N_HEADS, HEAD_W)
    pt_flat = page_table.reshape(-1)

    xs = jnp.transpose(x_sample, (1, 0, 2)).reshape(ns_tok, D)
    xp = x_prompt

    half = (np.arange(HEAD_W) // 64)[None, :] == np.arange(2)[:, None]
    qmask = jnp.asarray(half.astype(np.float32))[None, None, :, None, :]

    kp_l, vp_l, ks_l, vs_l, cp_l, cs_l, gs_l = [], [], [], [], [], [], []
    for l in range(depth):
        lam_init = 0.8 - 0.6 * math.exp(-0.3 * l)
        win_b = w_in[l].astype(BF16)
        wout_b = w_out[l].astype(BF16)
        wq_b = peer_wq[l].astype(BF16)
        wpw_b = w_conv_pw[l].astype(BF16)
        u_b = peer_u[l].astype(BF16)
        v_b = peer_v[l].astype(BF16)
        hw = peer_keys.shape[-1]
        kz = jnp.zeros(peer_keys.shape[1:4] + (LANES - hw,), F32)
        keys_pad = jnp.stack([jnp.concatenate([peer_keys[l, :, 0], kz[:, 0]], axis=-1),
                              jnp.concatenate([kz[:, 1], peer_keys[l, :, 1]], axis=-1)], axis=1)
        keys_pad = keys_pad.reshape(2 * PEER_HEADS, N_KEYS, LANES).astype(BF16)
        lam4 = jnp.stack([lam_q1[l], lam_k1[l], lam_q2[l], lam_k2[l]])
        g1 = norm1_g[l][None, :]
        g2 = norm2_g[l][None, :]
        gd = diff_norm_g[l][None, :]
        sgug = sgu_norm_g[l][None, :]
        cb = conv_b[l][None, :]
        cg = conv_norm_g[l][None, :]
        bs_full = jnp.repeat(b_s[l].T, gw, axis=1)
        wsx = jnp.repeat(jnp.transpose(w_s[l][:, :S, :S], (1, 2, 0)), gw, axis=2)
        bsx = bs_full[:S]

        qa, qb, kf, kb, vf, vb, osg, oc, crow = _proj_prompt(
            xp, g1, win_b, cos_p, sin_p, sgug, w_s[l], bs_full, conv_w[l], cb, cg, wpw_b, bd, tm_p)
        oa = _attn_prompt(qa, qb, kb, vb, lam4, gd, lam_init, tq, tk)
        x1, xn, qp = _post(xp.reshape(np_tok, D), oa.reshape(np_tok, aw), osg.reshape(np_tok, cwid),
                           oc.reshape(np_tok, cwid), wout_b, g2, wq_b, tm_post_p)
        xp = _peer(x1, xn, qp, keys_pad, u_b, v_b, tm_post_p, ec).reshape(B, T, D)
        kp_l.append(kf.reshape(B, T, N_HEADS, HEAD_W))
        vp_l.append(vf.reshape(B, T, N_HEADS, HEAD_W))
        cp_l.append(crow)

        prefix = jnp.transpose(state_conv[l], (1, 0, 2))
        qs, ksf, vsf, osg_s, oc_s, h_s, vn_s = _proj_sample(
            xs, g1, win_b, cos_s, sin_s, sgug, wsx, bsx, prefix, conv_w[l], cb, cg, wpw_b, bd)
        to_b = lambda a: jnp.transpose(a.reshape(S, DB, a.shape[-1]), (1, 0, 2))
        q_b, k_b, v_b2 = to_b(qs), to_b(ksf), to_b(vsf)
        heads = lambda a: jnp.transpose(a.reshape(DB, S, N_HEADS, HEAD_W), (0, 2, 1, 3))
        q32 = (heads(q_b)[:, :, None] * qmask).reshape(DB, 2 * N_HEADS * S, HEAD_W)
        rep = lambda a: jnp.repeat(a.reshape(DB, S, N_HEADS, HEAD_W), 2 * S, axis=2)
        oa_s32 = _attn_sample(pt_flat, q32, rep(k_b), rep(v_b2), cache_k2, cache_v2, lam4, gd, lam_init,
                              l * n_phys, n_pages, ppb)
        oa_s = oa_s32.reshape(DB, N_HEADS, 2, S, HEAD_W)[:, :, 0]
        oa_s = jnp.transpose(oa_s, (2, 0, 1, 3)).reshape(ns_tok, aw)
        x1s, xns, qps = _post(xs, oa_s, osg_s, oc_s, wout_b, g2, wq_b, tm_post_s)
        xs = _peer(x1s, xns, qps, keys_pad, u_b, v_b, tm_post_s, ec)
        ks_l.append(k_b.reshape(DB, S, N_HEADS, HEAD_W))
        vs_l.append(v_b2.reshape(DB, S, N_HEADS, HEAD_W))
        cs_l.append(jnp.concatenate([state_conv[l][:, S:], to_b(h_s)], axis=1))
        gs_l.append(to_b(vn_s))

    fg = final_norm_g[None, :]
    y_prompt = _final_norm(xp.reshape(np_tok, D), fg, tm_post_p).reshape(B, T, D)
    y_sample = jnp.transpose(_final_norm(xs, fg, tm_post_s).reshape(S, DB, D), (1, 0, 2))
    return (y_prompt, y_sample, jnp.stack(kp_l), jnp.stack(vp_l), jnp.stack(ks_l), jnp.stack(vs_l),
            jnp.stack(cp_l), jnp.stack(cs_l), jnp.stack(gs_l))
```

```python
import functools
import math

import numpy as np
import jax
import jax.numpy as jnp
from jax import lax
from jax.experimental import pallas as pl
from jax.experimental.pallas import tpu as pltpu

F32 = jnp.float32
BF16 = jnp.bfloat16

LANES = 128
SUBLANES = 8
EPS = 1e-6
NEG_INF = -1e30
ROPE_THETA = 10000.0

N_HEADS = 4
HEAD_W = 128
SGU_GROUPS = 4
CONV_GROUPS = 4
CONV_K = 31
CHUNK = 128
PAGE = 128
PEER_HEADS = 8
N_KEYS = 128
TOPK = 16

VMEM_LIMIT = 56 * 1024 * 1024

TM_PROJ = 512
TQ_ATTN = 1024
TM_TOKENS = 512
EXPERT_BLOCK = 1024
PEER_SLABS = 2


def _cparams(sem):
    return pltpu.CompilerParams(dimension_semantics=sem, vmem_limit_bytes=VMEM_LIMIT)


def _full(shape):
    n = len(shape)
    return pl.BlockSpec(shape, lambda *_: (0,) * n)


def _sigmoid(x):
    return 1.0 / (1.0 + jnp.exp(-x))


def _rms_rows(x, g):
    return x * lax.rsqrt(jnp.mean(x * x, axis=-1, keepdims=True) + EPS) * g


def _group_mean_sq(x, bd, group):
    x2 = x * x
    hi = x2.astype(BF16)
    lo = (x2 - hi.astype(F32)).astype(BF16)
    s = jnp.dot(hi, bd, preferred_element_type=F32) + jnp.dot(lo, bd, preferred_element_type=F32)
    return s * (1.0 / group)


def _rope_block(xh, cos, sin, lo_half):
    sw = jnp.where(lo_half, pltpu.roll(xh, LANES - 32, 1), pltpu.roll(xh, 32, 1))
    return xh * cos + sw * sin


def _lam_from(lam4):
    a = jnp.sum(lam4[0:1, :] * lam4[1:2, :], axis=-1, keepdims=True)
    b = jnp.sum(lam4[2:3, :] * lam4[3:4, :], axis=-1, keepdims=True)
    return jnp.exp(a) - jnp.exp(b)


def _proj_p_kernel(x_ref, g1_ref, win_ref, cos_ref, sin_ref, sgug_ref, ws_ref, bs_ref,
                   cw_ref, cb_ref, cg_ref, wpw_ref, bd_ref,
                   qa_ref, qb_ref, kf_ref, kb_ref, vf_ref, vb_ref, osgu_ref, oconv_ref, crow_ref,
                   hbuf):
    t = pl.program_id(1)
    tm = x_ref.shape[0]
    x = x_ref[...]
    hn = _rms_rows(x, g1_ref[...]).astype(BF16)
    p = jnp.dot(hn, win_ref[...], preferred_element_type=F32)
    aw = N_HEADS * HEAD_W
    q = p[:, 0:aw]
    k = p[:, aw:2 * aw]
    v = p[:, 2 * aw:3 * aw]
    o = 3 * aw
    cw = osgu_ref.shape[1]
    su = p[:, o:o + cw]
    sv = p[:, o + cw:o + 2 * cw]
    ca = p[:, o + 2 * cw:o + 3 * cw]
    cg = p[:, o + 3 * cw:o + 4 * cw]

    cos = cos_ref[...]
    sin = sin_ref[...]
    lane = lax.broadcasted_iota(jnp.int32, (tm, LANES), 1)
    lo_half = (lane & 32) == 0
    first = (lane & 64) == 0
    for h in range(N_HEADS):
        sl = slice(h * HEAD_W, (h + 1) * HEAD_W)
        qr = _rope_block(q[:, sl], cos, sin, lo_half) * 0.125
        qa_ref[:, sl] = jnp.where(first, qr, 0.0).astype(BF16)
        qb_ref[:, sl] = jnp.where(first, 0.0, qr).astype(BF16)
        kr = _rope_block(k[:, sl], cos, sin, lo_half)
        kf_ref[:, sl] = kr
        kb_ref[:, sl] = kr.astype(BF16)
    vf_ref[...] = v
    vb_ref[...] = v.astype(BF16)

    bd = bd_ref[...]
    u = jax.nn.gelu(su)
    vg = jax.nn.gelu(sv)
    vn = vg * lax.rsqrt(_group_mean_sq(vg, bd, cw // SGU_GROUPS) + EPS) * sgug_ref[...]
    vnb = vn.astype(BF16)
    r_i = lax.broadcasted_iota(jnp.int32, (CHUNK, CHUNK), 0)
    c_i = lax.broadcasted_iota(jnp.int32, (CHUNK, CHUNK), 1)
    tril = r_i >= c_i
    ws = [jnp.where(tril, ws_ref[g], 0.0).astype(BF16) for g in range(SGU_GROUPS)]
    left = c_i < 64
    bias = bs_ref[...]
    for c in range(tm // CHUNK):
        rows = slice(c * CHUNK, (c + 1) * CHUNK)
        parts = []
        for gp in range(SGU_GROUPS // 2):
            r = vnb[rows, gp * LANES:(gp + 1) * LANES]
            parts.append(jnp.where(left,
                                   jnp.dot(ws[2 * gp], r, preferred_element_type=F32),
                                   jnp.dot(ws[2 * gp + 1], r, preferred_element_type=F32)))
        mixed = jnp.concatenate(parts, axis=1) + bias
        osgu_ref[rows, :] = (u[rows, :] * mixed).astype(BF16)

    hg = ca * _sigmoid(cg)

    @pl.when(t == 0)
    def _():
        hbuf[0:32, :] = jnp.zeros((32, cw), F32)

    hbuf[32:32 + tm, :] = hg
    y = jnp.zeros((tm, cw), F32) + cb_ref[...]
    for j in range(CONV_K):
        y = y + hbuf[pl.ds(2 + j, tm), :] * cw_ref[j:j + 1, :]
    yn = y * lax.rsqrt(_group_mean_sq(y, bd, cw // CONV_GROUPS) + EPS) * cg_ref[...]
    ys = yn * _sigmoid(yn)
    oconv_ref[...] = jnp.dot(ys.astype(BF16), wpw_ref[...], preferred_element_type=F32).astype(BF16)
    crow_ref[...] = hbuf[pl.ds(tm + 2, CONV_K - 1), :]
    hbuf[0:32, :] = hbuf[tm:tm + 32, :]


def _proj_prompt(x, g1, win_b, cos, sin, sgug, ws, bs_full, cw, cb, cg, wpw_b, bd, tm):
    B, T, D = x.shape
    PW = win_b.shape[1]
    aw = N_HEADS * HEAD_W
    cwid = sgug.shape[1]
    row = lambda w: pl.BlockSpec((None, tm, w), lambda b, t: (b, t, 0))
    outs = (
        jax.ShapeDtypeStruct((B, T, aw), BF16), jax.ShapeDtypeStruct((B, T, aw), BF16),
        jax.ShapeDtypeStruct((B, T, aw), F32), jax.ShapeDtypeStruct((B, T, aw), BF16),
        jax.ShapeDtypeStruct((B, T, aw), F32), jax.ShapeDtypeStruct((B, T, aw), BF16),
        jax.ShapeDtypeStruct((B, T, cwid), BF16), jax.ShapeDtypeStruct((B, T, cwid), BF16),
        jax.ShapeDtypeStruct((B, CONV_K - 1, cwid), F32),
    )
    return pl.pallas_call(
        _proj_p_kernel,
        out_shape=outs,
        grid=(B, T // tm),
        in_specs=[row(D), _full((1, D)), _full((D, PW)),
                  pl.BlockSpec((tm, LANES), lambda b, t: (t, 0)),
                  pl.BlockSpec((tm, LANES), lambda b, t: (t, 0)),
                  _full((1, cwid)), _full(ws.shape), _full(bs_full.shape),
                  _full(cw.shape), _full((1, cwid)), _full((1, cwid)), _full(wpw_b.shape), _full(bd.shape)],
        out_specs=(row(aw), row(aw), row(aw), row(aw), row(aw), row(aw), row(cwid), row(cwid),
                   pl.BlockSpec((None, CONV_K - 1, cwid), lambda b, t: (b, 0, 0))),
        scratch_shapes=[pltpu.VMEM((tm + 32, cwid), F32)],
        compiler_params=_cparams(("arbitrary", "arbitrary")),
        name="proj_prompt",
    )(x, g1, win_b, cos, sin, sgug, ws, bs_full, cw, cb, cg, wpw_b, bd)


def _proj_s_kernel(x_ref, g1_ref, win_ref, cos_ref, sin_ref, sgug_ref, wsx_ref, bsx_ref,
                   pre_ref, cw_ref, cb_ref, cg_ref, wpw_ref, bd_ref,
                   q_ref, kf_ref, vf_ref, osgu_ref, oconv_ref, h_ref, vn_ref):
    n = x_ref.shape[0]
    nb = pre_ref.shape[1]
    S = n // nb
    x = x_ref[...]
    hn = _rms_rows(x, g1_ref[...]).astype(BF16)
    p = jnp.dot(hn, win_ref[...], preferred_element_type=F32)
    aw = N_HEADS * HEAD_W
    q = p[:, 0:aw]
    k = p[:, aw:2 * aw]
    o = 3 * aw
    cw = osgu_ref.shape[1]
    su = p[:, o:o + cw]
    sv = p[:, o + cw:o + 2 * cw]
    ca = p[:, o + 2 * cw:o + 3 * cw]
    cg = p[:, o + 3 * cw:o + 4 * cw]
    cos = cos_ref[...]
    sin = sin_ref[...]
    lane = lax.broadcasted_iota(jnp.int32, (n, LANES), 1)
    lo_half = (lane & 32) == 0
    for h in range(N_HEADS):
        sl = slice(h * HEAD_W, (h + 1) * HEAD_W)
        q_ref[:, sl] = _rope_block(q[:, sl], cos, sin, lo_half) * 0.125
        kf_ref[:, sl] = _rope_block(k[:, sl], cos, sin, lo_half)
    vf_ref[...] = p[:, 2 * aw:3 * aw]

    bd = bd_ref[...]
    u = jax.nn.gelu(su)
    vg = jax.nn.gelu(sv)
    vn = vg * lax.rsqrt(_group_mean_sq(vg, bd, cw // SGU_GROUPS) + EPS) * sgug_ref[...]
    vn_ref[...] = vn
    for t in range(S):
        mixed = jnp.zeros((nb, cw), F32) + bsx_ref[t:t + 1, :]
        for s in range(t + 1):
            mixed = mixed + wsx_ref[t, s:s + 1, :] * vn[s * nb:(s + 1) * nb, :]
        osgu_ref[t * nb:(t + 1) * nb, :] = (u[t * nb:(t + 1) * nb, :] * mixed).astype(BF16)

    hg = ca * _sigmoid(cg)
    h_ref[...] = hg
    npre = CONV_K - 1
    ys = []
    for t in range(S):
        y = jnp.zeros((nb, cw), F32) + cb_ref[...]
        for j in range(CONV_K):
            kk = t + j
            src = pre_ref[kk] if kk < npre else hg[(kk - npre) * nb:(kk - npre + 1) * nb, :]
            y = y + src * cw_ref[j:j + 1, :]
        ys.append(y)
    y = jnp.concatenate(ys, axis=0)
    yn = y * lax.rsqrt(_group_mean_sq(y, bd, cw // CONV_GROUPS) + EPS) * cg_ref[...]
    ysw = yn * _sigmoid(yn)
    oconv_ref[...] = jnp.dot(ysw.astype(BF16), wpw_ref[...], preferred_element_type=F32).astype(BF16)


def _proj_sample(x, g1, win_b, cos, sin, sgug, wsx, bsx, prefix, cw, cb, cg, wpw_b, bd):
    n, D = x.shape
    aw = N_HEADS * HEAD_W
    cwid = sgug.shape[1]
    args = (x, g1, win_b, cos, sin, sgug, wsx, bsx, prefix, cw, cb, cg, wpw_b, bd)
    outs = (jax.ShapeDtypeStruct((n, aw), F32), jax.ShapeDtypeStruct((n, aw), F32),
            jax.ShapeDtypeStruct((n, aw), F32),
            jax.ShapeDtypeStruct((n, cwid), BF16), jax.ShapeDtypeStruct((n, cwid), BF16),
            jax.ShapeDtypeStruct((n, cwid), F32), jax.ShapeDtypeStruct((n, cwid), F32))
    return pl.pallas_call(
        _proj_s_kernel,
        out_shape=outs,
        grid=(1,),
        in_specs=[_full(a.shape) for a in args],
        out_specs=tuple(_full(o.shape) for o in outs),
        compiler_params=_cparams(("arbitrary",)),
        name="proj_sample",
    )(*args)


def _attn_p_kernel(qt_ref, kt_ref, lam4_ref, g_ref, qa_ref, qb_ref, k_ref, v_ref, o_ref,
                   m0, l0, a0, m1, l1, a1, *, lam_init):
    s_id = pl.program_id(2)
    qi = qt_ref[s_id]
    ki = kt_ref[s_id]
    tq = qa_ref.shape[0]
    tk = k_ref.shape[0]

    @pl.when(ki == 0)
    def _():
        for m, l, a in ((m0, l0, a0), (m1, l1, a1)):
            m[...] = jnp.full(m.shape, NEG_INF, F32)
            l[...] = jnp.zeros(l.shape, F32)
            a[...] = jnp.zeros(a.shape, F32)

    kk = k_ref[...]
    vv = v_ref[...]
    qpos = qi * tq + lax.broadcasted_iota(jnp.int32, (tq, tk), 0)
    kpos = ki * tk + lax.broadcasted_iota(jnp.int32, (tq, tk), 1)
    keep = kpos <= qpos
    for q_ref, m, l, a in ((qa_ref, m0, l0, a0), (qb_ref, m1, l1, a1)):
        s = lax.dot_general(q_ref[...], kk, (((1,), (1,)), ((), ())), preferred_element_type=F32)
        s = jnp.where(keep, s, NEG_INF)
        m_old = m[...]
        m_new = jnp.maximum(m_old, jnp.max(s, axis=-1, keepdims=True))
        alpha = jnp.exp(m_old - m_new)
        p = jnp.exp(s - m_new)
        l[...] = alpha * l[...] + jnp.sum(p, axis=-1, keepdims=True)
        a[...] = alpha * a[...] + jnp.dot(p.astype(BF16), vv, preferred_element_type=F32)
        m[...] = m_new

    @pl.when(ki == (qi * tq + tq - 1) // tk)
    def _():
        lam = _lam_from(lam4_ref[...]) + lam_init
        o = a0[...] / l0[...] - lam * (a1[...] / l1[...])
        o_ref[...] = (_rms_rows(o, g_ref[...]) * (1.0 - lam_init)).astype(o_ref.dtype)


def _attn_prompt(qa, qb, kb, vb, lam4, g, lam_init, tq, tk):
    B, T, aw = qa.shape
    nq = T // tq
    qs, ks = [], []
    for i in range(nq):
        last = (i * tq + tq - 1) // tk
        for j in range(last + 1):
            qs.append(i)
            ks.append(j)
    qt = jnp.asarray(np.array(qs, np.int32))
    kt = jnp.asarray(np.array(ks, np.int32))
    qspec = pl.BlockSpec((None, tq, HEAD_W), lambda b, h, s, qt, kt: (b, qt[s], h))
    kspec = pl.BlockSpec((None, tk, HEAD_W), lambda b, h, s, qt, kt: (b, kt[s], h))
    grid_spec = pltpu.PrefetchScalarGridSpec(
        num_scalar_prefetch=2,
        grid=(B, N_HEADS, len(qs)),
        in_specs=[pl.BlockSpec(lam4.shape, lambda b, h, s, qt, kt: (0, 0)),
                  pl.BlockSpec(g.shape, lambda b, h, s, qt, kt: (0, 0)),
                  qspec, qspec, kspec, kspec],
        out_specs=qspec,
        scratch_shapes=[pltpu.VMEM((tq, 1), F32), pltpu.VMEM((tq, 1), F32), pltpu.VMEM((tq, HEAD_W), F32),
                        pltpu.VMEM((tq, 1), F32), pltpu.VMEM((tq, 1), F32), pltpu.VMEM((tq, HEAD_W), F32)],
    )
    return pl.pallas_call(
        functools.partial(_attn_p_kernel, lam_init=lam_init),
        out_shape=jax.ShapeDtypeStruct((B, T, aw), BF16),
        grid_spec=grid_spec,
        compiler_params=_cparams(("arbitrary", "arbitrary", "arbitrary")),
        name="attn_prompt",
    )(qt, kt, lam4, g, qa, qb, kb, vb)


def _attn_s_kernel(pt_ref, lam4_ref, g_ref, q_ref, knx_ref, vnx_ref, *rest, lam_init, ppb, n_new):
    k_refs = rest[:ppb]
    v_refs = rest[ppb:2 * ppb]
    o_ref = rest[2 * ppb]
    m_sc, l_sc, a_sc = rest[2 * ppb + 1:]
    j = pl.program_id(1)
    nj = pl.num_programs(1)

    @pl.when(j == 0)
    def _():
        m_sc[...] = jnp.full(m_sc.shape, NEG_INF, F32)
        l_sc[...] = jnp.zeros(l_sc.shape, F32)
        a_sc[...] = jnp.zeros(a_sc.shape, F32)

    q = q_ref[...]
    qb = q.astype(BF16)
    rows = q.shape[0]
    ncol = k_refs[0].shape[0]
    col = lax.broadcasted_iota(jnp.int32, (rows, ncol), 1)
    row = lax.broadcasted_iota(jnp.int32, (rows, ncol), 0)
    own = (col & (N_HEADS - 1)) == (row >> 3)
    ss = []
    for kr in k_refs:
        kk = kr[...].astype(BF16)
        s = lax.dot_general(qb, kk, (((1,), (1,)), ((), ())), preferred_element_type=F32)
        ss.append(jnp.where(own, s, NEG_INF))
    m_old = m_sc[...]
    m_new = m_old
    for s in ss:
        m_new = jnp.maximum(m_new, jnp.max(s, axis=-1, keepdims=True))
    alpha = jnp.exp(m_old - m_new)
    l = alpha * l_sc[...]
    acc = alpha * a_sc[...]
    for s, vr in zip(ss, v_refs):
        p = jnp.exp(s - m_new)
        l = l + jnp.sum(p, axis=-1, keepdims=True)
        acc = acc + jnp.dot(p.astype(BF16), vr[...].astype(BF16), preferred_element_type=F32)
    l_sc[...] = l
    a_sc[...] = acc
    m_sc[...] = m_new

    @pl.when(j == nj - 1)
    def _():
        tok = lax.broadcasted_iota(jnp.int32, (rows, 1), 0) & (n_new - 1)
        sn = []
        for c in range(n_new):
            sc = jnp.sum(q * knx_ref[c], axis=-1, keepdims=True)
            sn.append(jnp.where(tok >= c, sc, NEG_INF))
        m_old = m_sc[...]
        m_new = m_old
        for sc in sn:
            m_new = jnp.maximum(m_new, sc)
        alpha = jnp.exp(m_old - m_new)
        l = alpha * l_sc[...]
        acc = alpha * a_sc[...]
        for c, sc in enumerate(sn):
            pc = jnp.exp(sc - m_new)
            l = l + pc
            acc = acc + pc * vnx_ref[c]
        xo = acc / l
        lam = _lam_from(lam4_ref[...]) + lam_init
        o = xo - lam * pltpu.roll(xo, rows - n_new, 0)
        o_ref[...] = _rms_rows(o, g_ref[...]) * (1.0 - lam_init)


def _attn_sample(pt_flat, q32, knx, vnx, cache_k2, cache_v2, lam4, g, lam_init, layer_base, n_pages, ppb):
    nb = q32.shape[0]
    n_new = knx.shape[1]
    steps = n_pages // ppb
    prow = PAGE * N_HEADS

    def page_spec(p):
        return pl.BlockSpec((prow, HEAD_W),
                            lambda b, j, pt, p=p: (layer_base + pt[b * n_pages + j * ppb + p], 0))

    grid_spec = pltpu.PrefetchScalarGridSpec(
        num_scalar_prefetch=1,
        grid=(nb, steps),
        in_specs=[pl.BlockSpec(lam4.shape, lambda b, j, pt: (0, 0)),
                  pl.BlockSpec(g.shape, lambda b, j, pt: (0, 0)),
                  pl.BlockSpec((None, 32, HEAD_W), lambda b, j, pt: (b, 0, 0)),
                  pl.BlockSpec((None, n_new, 32, HEAD_W), lambda b, j, pt: (b, 0, 0, 0)),
                  pl.BlockSpec((None, n_new, 32, HEAD_W), lambda b, j, pt: (b, 0, 0, 0))]
                 + [page_spec(p) for p in range(ppb)] + [page_spec(p) for p in range(ppb)],
        out_specs=pl.BlockSpec((None, 32, HEAD_W), lambda b, j, pt: (b, 0, 0)),
        scratch_shapes=[pltpu.VMEM((32, 1), F32), pltpu.VMEM((32, 1), F32), pltpu.VMEM((32, HEAD_W), F32)],
    )
    return pl.pallas_call(
        functools.partial(_attn_s_kernel, lam_init=lam_init, ppb=ppb, n_new=n_new),
        out_shape=jax.ShapeDtypeStruct((nb, 32, HEAD_W), F32),
        grid_spec=grid_spec,
        compiler_params=_cparams(("arbitrary", "arbitrary")),
        name="attn_sample",
    )(pt_flat, lam4, g, q32, knx, vnx, *([cache_k2] * ppb), *([cache_v2] * ppb))


def _post_kernel(x_ref, oa_ref, os_ref, oc_ref, wout_ref, g2_ref, wq_ref, x1_ref, xn_ref, qp_ref):
    aw = oa_ref.shape[1]
    cw = os_ref.shape[1]
    x1 = (x_ref[...]
          + jnp.dot(oa_ref[...].astype(BF16), wout_ref[0:aw, :], preferred_element_type=F32)
          + jnp.dot(os_ref[...], wout_ref[aw:aw + cw, :], preferred_element_type=F32)
          + jnp.dot(oc_ref[...], wout_ref[aw + cw:aw + 2 * cw, :], preferred_element_type=F32))
    x1_ref[...] = x1
    xn = _rms_rows(x1, g2_ref[...]).astype(BF16)
    xn_ref[...] = xn
    qp = jnp.dot(xn, wq_ref[...], preferred_element_type=F32).astype(BF16)
    for h in range(PEER_HEADS):
        qp_ref[h] = qp[:, h * LANES:(h + 1) * LANES]


def _post(x, oa, osg, oc, wout_b, g2, wq_b, tm):
    n, D = x.shape
    row = lambda w: pl.BlockSpec((tm, w), lambda i: (i, 0))
    return pl.pallas_call(
        _post_kernel,
        out_shape=(jax.ShapeDtypeStruct((n, D), F32), jax.ShapeDtypeStruct((n, D), BF16),
                   jax.ShapeDtypeStruct((PEER_HEADS, n, LANES), BF16)),
        grid=(n // tm,),
        in_specs=[row(D), row(oa.shape[1]), row(osg.shape[1]), row(oc.shape[1]),
                  _full(wout_b.shape), _full((1, D)), _full(wq_b.shape)],
        out_specs=(row(D), row(D), pl.BlockSpec((PEER_HEADS, tm, LANES), lambda i: (0, i, 0))),
        compiler_params=_cparams(("arbitrary",)),
        name="post",
    )(x, oa, osg, oc, wout_b, g2, wq_b)


def _sort_pairs(n):
    pairs = []
    p = 1
    while p < n:
        k = p
        while k >= 1:
            for j in range(k % p, n - k, 2 * k):
                for i in range(min(k, n - j - k)):
                    if (i + j) // (2 * p) == (i + j + k) // (2 * p):
                        pairs.append((i + j, i + j + k))
            k //= 2
        p *= 2
    return pairs


def _cmpx(lst, i, j):
    a, b = lst[i], lst[j]
    if b is None:
        return
    if a is None:
        lst[i], lst[j] = b, None
        return
    lst[i], lst[j] = jnp.maximum(a, b), jnp.minimum(a, b)


def _sort_desc(vals):
    n = 1
    while n < len(vals):
        n *= 2
    lst = list(vals) + [None] * (n - len(vals))
    for i, j in _sort_pairs(n):
        _cmpx(lst, i, j)
    return lst


def _top16_rows(sT):
    lst = _sort_desc([sT[SUBLANES * v:SUBLANES * (v + 1), :] for v in range(N_KEYS // SUBLANES)])
    for shift in (4, 2, 1):
        other = [pltpu.roll(x, shift, 0) for x in lst]
        lst = [jnp.maximum(lst[i], other[TOPK - 1 - i]) for i in range(TOPK)]
        d = TOPK // 2
        while d >= 1:
            for i in range(TOPK):
                if (i & d) == 0:
                    _cmpx(lst, i, i + d)
            d //= 2
    return lst


def _peer_select(q, k1, k2):
    dn = (((1,), (1,)), ((), ()))
    s1 = lax.dot_general(k1, q, dn, preferred_element_type=F32)
    s2 = lax.dot_general(k2, q, dn, preferred_element_type=F32)
    a = _top16_rows(s1)
    b = _top16_rows(s2)
    cand = {}
    for r in range(TOPK):
        for c in range(TOPK // (r + 1)):
            cand[(r, c)] = a[r] + b[c]
    tau = _sort_desc(list(cand.values()))[TOPK - 1]
    m = cand[(0, 0)]
    z = jnp.zeros(tau.shape, F32)
    len_r = []
    for r in range(TOPK):
        cnt = jnp.zeros(tau.shape, F32)
        for c in range(TOPK // (r + 1)):
            sel = cand[(r, c)] >= tau
            cnt = cnt + jnp.where(sel, 1.0, 0.0)
            z = z + jnp.where(sel, jnp.exp(cand[(r, c)] - m), 0.0)
        len_r.append(cnt)
    zinv = 1.0 / z
    nv = N_KEYS // SUBLANES
    len1, e1, rank2, e2 = [], [], [], []
    for v in range(nv):
        rows = s1[SUBLANES * v:SUBLANES * (v + 1), :]
        ln = jnp.zeros(tau.shape, F32)
        for r in range(TOPK - 1, -1, -1):
            ln = jnp.where(rows >= a[r], len_r[r], ln)
        len1.append(ln)
        e1.append(jnp.exp(rows - a[0]))
        rows2 = s2[SUBLANES * v:SUBLANES * (v + 1), :]
        rk = jnp.full(tau.shape, float(TOPK), F32)
        for c in range(TOPK - 1, -1, -1):
            rk = jnp.where(rows2 >= b[c], float(c), rk)
        rank2.append(rk)
        e2.append(jnp.exp(rows2 - b[0]) * zinv)
    cat = lambda xs: jnp.concatenate(xs, axis=0)
    return cat(len1), cat(e1), cat(rank2).astype(BF16), cat(e2).astype(BF16)


def _peer_kernel(x1_ref, xn_ref, qp_ref, keys_ref, u_ref, vprev_ref, vlast_ref, o_ref,
                 ln_sc, e1_sc, r2_sc, e2_sc, at_sc, ht_sc, acc_sc, *, nslab):
    e = pl.program_id(1)
    ne = pl.num_programs(1)
    tm = xn_ref.shape[0]
    ec = u_ref.shape[0]
    ntb = tm // LANES
    ni = ec // N_KEYS
    per = ni // nslab
    slot = e % 2
    dn_t = (((0,), (0,)), ((), ()))

    @pl.when(e == 0)
    def _():
        acc_sc[...] = jnp.zeros(acc_sc.shape, F32)
        ht_sc[1] = jnp.zeros(ht_sc.shape[1:], BF16)

        def body(it, carry):
            h = it // ntb
            tb = it % ntb
            q = qp_ref[h, pl.ds(pl.multiple_of(tb * LANES, LANES), LANES), :]
            ln, e1, r2, e2 = _peer_select(q, keys_ref[2 * h], keys_ref[2 * h + 1])
            ln_sc[h, tb] = ln
            e1_sc[h, tb] = e1
            r2_sc[h, tb] = r2
            e2_sc[h, tb] = e2
            return carry

        lax.fori_loop(0, PEER_HEADS * ntb, body, 0)

    at_sc[...] = lax.dot_general(u_ref[...], xn_ref[...], (((1,), (1,)), ((), ())),
                                 preferred_element_type=F32)

    def chunk(c, carry):
        acc_sc[c] += lax.dot_general(ht_sc[1 - slot], vprev_ref[c], dn_t, preferred_element_type=F32)
        for k in range(per):
            il = c * per + k
            ig = e * ni + il
            r0 = pl.multiple_of(il * N_KEYS, N_KEYS)
            for tb in range(ntb):
                cols = slice(tb * LANES, (tb + 1) * LANES)
                act = jax.nn.gelu(at_sc[pl.ds(r0, N_KEYS), cols]).astype(BF16)
                gate = jnp.zeros((N_KEYS, LANES), BF16)
                for h in range(PEER_HEADS):
                    ln = ln_sc[h, tb, pl.ds(ig, 1), :].astype(BF16)
                    e1 = e1_sc[h, tb, pl.ds(ig, 1), :].astype(BF16)
                    gate = gate + jnp.where(r2_sc[h, tb] < ln, e2_sc[h, tb], jnp.zeros((), BF16)) * e1
                ht_sc[slot, pl.ds(r0, N_KEYS), cols] = act * gate
        return carry

    lax.fori_loop(0, nslab, chunk, 0)

    @pl.when(e == ne - 1)
    def _():
        w = o_ref.shape[1] // nslab
        for c in range(nslab):
            o_ref[:, c * w:(c + 1) * w] = (x1_ref[:, c * w:(c + 1) * w] + acc_sc[c]
                                           + lax.dot_general(ht_sc[slot], vlast_ref[c], dn_t,
                                                             preferred_element_type=F32))


def _peer(x1, xn, qp, keys_pad, u_b, v_s, tm, ec):
    n, D = x1.shape
    ne, nslab, _, w = v_s.shape
    assert ne % 2 == 0 and ne * ec == u_b.shape[0] and nslab * w == D
    ntb = tm // LANES
    row = lambda ww: pl.BlockSpec((tm, ww), lambda i, e: (i, 0))
    sel32 = pltpu.VMEM((PEER_HEADS, ntb, N_KEYS, LANES), F32)
    sel16 = pltpu.VMEM((PEER_HEADS, ntb, N_KEYS, LANES), BF16)
    return pl.pallas_call(
        functools.partial(_peer_kernel, nslab=nslab),
        out_shape=jax.ShapeDtypeStruct((n, D), F32),
        grid=(n // tm, ne),
        in_specs=[row(D), row(D),
                  pl.BlockSpec((PEER_HEADS, tm, LANES), lambda i, e: (0, i, 0)),
                  pl.BlockSpec(keys_pad.shape, lambda i, e: (0, 0, 0)),
                  pl.BlockSpec((ec, D), lambda i, e: (e, 0)),
                  pl.BlockSpec((None, nslab, ec, w), lambda i, e: (jnp.maximum(e - 1, 0), 0, 0, 0)),
                  pl.BlockSpec((None, nslab, ec, w), lambda i, e: (ne - 1, 0, 0, 0))],
        out_specs=row(D),
        scratch_shapes=[sel32, sel32, sel16, sel16,
                        pltpu.VMEM((ec, tm), F32), pltpu.VMEM((2, ec, tm), BF16),
                        pltpu.VMEM((nslab, tm, w), F32)],
        compiler_params=_cparams(("arbitrary", "arbitrary")),
        name="peer",
    )(x1, xn, qp, keys_pad, u_b, v_s, v_s)


def _final_kernel(x_ref, g_ref, o_ref):
    o_ref[...] = _rms_rows(x_ref[...], g_ref[...])


def _final_norm(x, g, tm):
    n, D = x.shape
    return pl.pallas_call(
        _final_kernel,
        out_shape=jax.ShapeDtypeStruct((n, D), F32),
        grid=(n // tm,),
        in_specs=[pl.BlockSpec((tm, D), lambda i: (i, 0)), _full((1, D))],
        out_specs=pl.BlockSpec((tm, D), lambda i: (i, 0)),
        compiler_params=_cparams(("arbitrary",)),
        name="final_norm",
    )(x, g)


def _rope_tables(pos):
    half = 32
    inv = ROPE_THETA ** (-jnp.arange(half, dtype=F32) / half)
    ang = pos.astype(F32)[:, None] * inv[None, :]
    c, s = jnp.cos(ang), jnp.sin(ang)
    cos = jnp.concatenate([c, c, c, c], axis=1)
    sin = jnp.concatenate([-s, s, -s, s], axis=1)
    return cos, sin


def _block_diag_ones(width, group):
    i = np.arange(width) // group
    return jnp.asarray((i[:, None] == i[None, :]).astype(np.float32), dtype=BF16)


def _pick_tile(n, pref):
    t = pref
    while n % t:
        t //= 2
    return t


def kernel(x_prompt, x_sample, cache_k, cache_v, state_conv, page_table, norm1_g, w_in, lam_q1, lam_k1,
           lam_q2, lam_k2, diff_norm_g, sgu_norm_g, w_s, b_s, conv_w, conv_b, conv_norm_g, w_conv_pw,
           w_out, norm2_g, peer_wq, peer_keys, peer_u, peer_v, final_norm_g):
    B, T, D = x_prompt.shape
    DB, S, _ = x_sample.shape
    depth = w_in.shape[0]
    n_phys = cache_k.shape[1]
    n_pages = page_table.shape[1]
    past = n_pages * PAGE
    aw = N_HEADS * HEAD_W
    cwid = sgu_norm_g.shape[1]
    gw = cwid // SGU_GROUPS
    n_exp = peer_u.shape[1]
    assert S == 4 and cwid == conv_norm_g.shape[1] and n_exp == N_KEYS * N_KEYS

    tm_p = _pick_tile(T, TM_PROJ)
    tq = _pick_tile(T, TQ_ATTN)
    tk = tq
    np_tok = B * T
    ns_tok = DB * S
    tm_post_p = _pick_tile(np_tok, TM_TOKENS)
    tm_post_s = _pick_tile(ns_tok, TM_TOKENS)
    ec = EXPERT_BLOCK
    ppb = next(p for p in (16, 8, 4, 2, 1) if n_pages % p == 0)

    cos_p, sin_p = _rope_tables(jnp.arange(T, dtype=jnp.int32))
    pos_s = past + jnp.repeat(jnp.arange(S, dtype=jnp.int32), DB)
    cos_s, sin_s = _rope_tables(pos_s)
    bd = _block_diag_ones(cwid, gw)

    cache_k2 = cache_k.reshape(depth * n_phys * PAGE * N_HEADS, HEAD_W)
    cache_v2 = cache_v.reshape(depth * n_phys * PAGE * N_HEADS, HEAD_W)
    pt_flat = page_table.reshape(-1)

    xs = jnp.transpose(x_sample, (1, 0, 2)).reshape(ns_tok, D)
    xp = x_prompt

    half = (np.arange(HEAD_W) // 64)[None, :] == np.arange(2)[:, None]
    qmask = jnp.asarray(half.astype(np.float32))[None, None, :, None, :]

    kp_l, vp_l, ks_l, vs_l, cp_l, cs_l, gs_l = [], [], [], [], [], [], []
    for l in range(depth):
        lam_init = 0.8 - 0.6 * math.exp(-0.3 * l)
        win_b = w_in[l].astype(BF16)
        wout_b = w_out[l].astype(BF16)
        wq_b = peer_wq[l].astype(BF16)
        wpw_b = w_conv_pw[l].astype(BF16)
        u_b = peer_u[l].astype(BF16)
        v_s = jnp.transpose(peer_v[l].astype(BF16).reshape(n_exp // ec, ec, PEER_SLABS, D // PEER_SLABS),
                            (0, 2, 1, 3))
        hw = peer_keys.shape[-1]
        kz = jnp.zeros(peer_keys.shape[1:4] + (LANES - hw,), F32)
        keys_pad = jnp.stack([jnp.concatenate([peer_keys[l, :, 0], kz[:, 0]], axis=-1),
                              jnp.concatenate([kz[:, 1], peer_keys[l, :, 1]], axis=-1)], axis=1)
        keys_pad = keys_pad.reshape(2 * PEER_HEADS, N_KEYS, LANES).astype(BF16)
        lam4 = jnp.stack([lam_q1[l], lam_k1[l], lam_q2[l], lam_k2[l]])
        g1 = norm1_g[l][None, :]
        g2 = norm2_g[l][None, :]
        gd = diff_norm_g[l][None, :]
        sgug = sgu_norm_g[l][None, :]
        cb = conv_b[l][None, :]
        cg = conv_norm_g[l][None, :]
        bs_full = jnp.repeat(b_s[l].T, gw, axis=1)
        wsx = jnp.repeat(jnp.transpose(w_s[l][:, :S, :S], (1, 2, 0)), gw, axis=2)
        bsx = bs_full[:S]

        qa, qb, kf, kb, vf, vb, osg, oc, crow = _proj_prompt(
            xp, g1, win_b, cos_p, sin_p, sgug, w_s[l], bs_full, conv_w[l], cb, cg, wpw_b, bd, tm_p)
        oa = _attn_prompt(qa, qb, kb, vb, lam4, gd, lam_init, tq, tk)
        x1, xn, qp = _post(xp.reshape(np_tok, D), oa.reshape(np_tok, aw), osg.reshape(np_tok, cwid),
                           oc.reshape(np_tok, cwid), wout_b, g2, wq_b, tm_post_p)
        xp = _peer(x1, xn, qp, keys_pad, u_b, v_s, tm_post_p, ec).reshape(B, T, D)
        kp_l.append(kf.reshape(B, T, N_HEADS, HEAD_W))
        vp_l.append(vf.reshape(B, T, N_HEADS, HEAD_W))
        cp_l.append(crow)

        prefix = jnp.transpose(state_conv[l], (1, 0, 2))
        qs, ksf, vsf, osg_s, oc_s, h_s, vn_s = _proj_sample(
            xs, g1, win_b, cos_s, sin_s, sgug, wsx, bsx, prefix, conv_w[l], cb, cg, wpw_b, bd)
        to_b = lambda a: jnp.transpose(a.reshape(S, DB, a.shape[-1]), (1, 0, 2))
        q_b, k_b, v_b2 = to_b(qs), to_b(ksf), to_b(vsf)
        heads = lambda a: jnp.transpose(a.reshape(DB, S, N_HEADS, HEAD_W), (0, 2, 1, 3))
        q32 = (heads(q_b)[:, :, None] * qmask).reshape(DB, 2 * N_HEADS * S, HEAD_W)
        rep = lambda a: jnp.repeat(a.reshape(DB, S, N_HEADS, HEAD_W), 2 * S, axis=2)
        oa_s32 = _attn_sample(pt_flat, q32, rep(k_b), rep(v_b2), cache_k2, cache_v2, lam4, gd, lam_init,
                              l * n_phys, n_pages, ppb)
        oa_s = oa_s32.reshape(DB, N_HEADS, 2, S, HEAD_W)[:, :, 0]
        oa_s = jnp.transpose(oa_s, (2, 0, 1, 3)).reshape(ns_tok, aw)
        x1s, xns, qps = _post(xs, oa_s, osg_s, oc_s, wout_b, g2, wq_b, tm_post_s)
        xs = _peer(x1s, xns, qps, keys_pad, u_b, v_s, tm_post_s, ec)
        ks_l.append(k_b.reshape(DB, S, N_HEADS, HEAD_W))
        vs_l.append(v_b2.reshape(DB, S, N_HEADS, HEAD_W))
        cs_l.append(jnp.concatenate([state_conv[l][:, S:], to_b(h_s)], axis=1))
        gs_l.append(to_b(vn_s))

    fg = final_norm_g[None, :]
    y_prompt = _final_norm(xp.reshape(np_tok, D), fg, tm_post_p).reshape(B, T, D)
    y_sample = jnp.transpose(_final_norm(xs, fg, tm_post_s).reshape(S, DB, D), (1, 0, 2))
    return (y_prompt, y_sample, jnp.stack(kp_l), jnp.stack(vp_l), jnp.stack(ks_l), jnp.stack(vs_l),
            jnp.stack(cp_l), jnp.stack(cs_l), jnp.stack(gs_l))
```

```python
import functools
import math

import numpy as np
import jax
import jax.numpy as jnp
from jax import lax
from jax.experimental import pallas as pl
from jax.experimental.pallas import tpu as pltpu

F32 = jnp.float32
BF16 = jnp.bfloat16

LANES = 128
SUBLANES = 8
EPS = 1e-6
NEG_INF = -1e30
ROPE_THETA = 10000.0

N_HEADS = 4
HEAD_W = 128
SGU_GROUPS = 4
CONV_GROUPS = 4
CONV_K = 31
CHUNK = 128
PAGE = 128
PEER_HEADS = 8
N_KEYS = 128
TOPK = 16

VMEM_LIMIT = 56 * 1024 * 1024

TM_PROJ = 512
TQ_ATTN = 1024
ATTN_ROW_PARTS = 4
TM_TOKENS = 512
EXPERT_BLOCK = 2048


def _cparams(sem):
    return pltpu.CompilerParams(dimension_semantics=sem, vmem_limit_bytes=VMEM_LIMIT)


def _full(shape):
    n = len(shape)
    return pl.BlockSpec(shape, lambda *_: (0,) * n)


def _sigmoid(x):
    return 1.0 / (1.0 + jnp.exp(-x))


def _rms_rows(x, g):
    return x * lax.rsqrt(jnp.mean(x * x, axis=-1, keepdims=True) + EPS) * g


def _group_mean_sq(x, bd, group):
    x2 = x * x
    hi = x2.astype(BF16)
    lo = (x2 - hi.astype(F32)).astype(BF16)
    s = jnp.dot(hi, bd, preferred_element_type=F32) + jnp.dot(lo, bd, preferred_element_type=F32)
    return s * (1.0 / group)


def _rope_block(xh, cos, sin, lo_half):
    sw = jnp.where(lo_half, pltpu.roll(xh, LANES - 32, 1), pltpu.roll(xh, 32, 1))
    return xh * cos + sw * sin


def _lam_from(lam4):
    a = jnp.sum(lam4[0:1, :] * lam4[1:2, :], axis=-1, keepdims=True)
    b = jnp.sum(lam4[2:3, :] * lam4[3:4, :], axis=-1, keepdims=True)
    return jnp.exp(a) - jnp.exp(b)


def _proj_p_kernel(x_ref, g1_ref, win_ref, cos_ref, sin_ref, sgug_ref, ws_ref, bs_ref,
                   cw_ref, cb_ref, cg_ref, wpw_ref, bd_ref,
                   qa_ref, qb_ref, kf_ref, kb_ref, vf_ref, vb_ref, osgu_ref, oconv_ref, crow_ref,
                   hbuf):
    t = pl.program_id(1)
    tm = x_ref.shape[0]
    x = x_ref[...]
    hn = _rms_rows(x, g1_ref[...]).astype(BF16)
    p = jnp.dot(hn, win_ref[...], preferred_element_type=F32)
    aw = N_HEADS * HEAD_W
    q = p[:, 0:aw]
    k = p[:, aw:2 * aw]
    v = p[:, 2 * aw:3 * aw]
    o = 3 * aw
    cw = osgu_ref.shape[1]
    su = p[:, o:o + cw]
    sv = p[:, o + cw:o + 2 * cw]
    ca = p[:, o + 2 * cw:o + 3 * cw]
    cg = p[:, o + 3 * cw:o + 4 * cw]

    cos = cos_ref[...]
    sin = sin_ref[...]
    lane = lax.broadcasted_iota(jnp.int32, (tm, LANES), 1)
    lo_half = (lane & 32) == 0
    first = (lane & 64) == 0
    for h in range(N_HEADS):
        sl = slice(h * HEAD_W, (h + 1) * HEAD_W)
        qr = _rope_block(q[:, sl], cos, sin, lo_half) * 0.125
        qa_ref[:, sl] = jnp.where(first, qr, 0.0).astype(BF16)
        qb_ref[:, sl] = jnp.where(first, 0.0, qr).astype(BF16)
        kr = _rope_block(k[:, sl], cos, sin, lo_half)
        kf_ref[:, sl] = kr
        kb_ref[:, sl] = kr.astype(BF16)
    vf_ref[...] = v
    vb_ref[...] = v.astype(BF16)

    bd = bd_ref[...]
    u = jax.nn.gelu(su)
    vg = jax.nn.gelu(sv)
    vn = vg * lax.rsqrt(_group_mean_sq(vg, bd, cw // SGU_GROUPS) + EPS) * sgug_ref[...]
    vnb = vn.astype(BF16)
    r_i = lax.broadcasted_iota(jnp.int32, (CHUNK, CHUNK), 0)
    c_i = lax.broadcasted_iota(jnp.int32, (CHUNK, CHUNK), 1)
    tril = r_i >= c_i
    ws = [jnp.where(tril, ws_ref[g], 0.0).astype(BF16) for g in range(SGU_GROUPS)]
    left = c_i < 64
    bias = bs_ref[...]
    for c in range(tm // CHUNK):
        rows = slice(c * CHUNK, (c + 1) * CHUNK)
        parts = []
        for gp in range(SGU_GROUPS // 2):
            r = vnb[rows, gp * LANES:(gp + 1) * LANES]
            parts.append(jnp.where(left,
                                   jnp.dot(ws[2 * gp], r, preferred_element_type=F32),
                                   jnp.dot(ws[2 * gp + 1], r, preferred_element_type=F32)))
        mixed = jnp.concatenate(parts, axis=1) + bias
        osgu_ref[rows, :] = (u[rows, :] * mixed).astype(BF16)

    hg = ca * _sigmoid(cg)

    @pl.when(t == 0)
    def _():
        hbuf[0:32, :] = jnp.zeros((32, cw), F32)

    hbuf[32:32 + tm, :] = hg
    y = jnp.zeros((tm, cw), F32) + cb_ref[...]
    for j in range(CONV_K):
        y = y + hbuf[pl.ds(2 + j, tm), :] * cw_ref[j:j + 1, :]
    yn = y * lax.rsqrt(_group_mean_sq(y, bd, cw // CONV_GROUPS) + EPS) * cg_ref[...]
    ys = yn * _sigmoid(yn)
    oconv_ref[...] = jnp.dot(ys.astype(BF16), wpw_ref[...], preferred_element_type=F32).astype(BF16)
    crow_ref[...] = hbuf[pl.ds(tm + 2, CONV_K - 1), :]
    hbuf[0:32, :] = hbuf[tm:tm + 32, :]


def _proj_prompt(x, g1, win_b, cos, sin, sgug, ws, bs_full, cw, cb, cg, wpw_b, bd, tm):
    B, T, D = x.shape
    PW = win_b.shape[1]
    aw = N_HEADS * HEAD_W
    cwid = sgug.shape[1]
    row = lambda w: pl.BlockSpec((None, tm, w), lambda b, t: (b, t, 0))
    outs = (
        jax.ShapeDtypeStruct((B, T, aw), BF16), jax.ShapeDtypeStruct((B, T, aw), BF16),
        jax.ShapeDtypeStruct((B, T, aw), F32), jax.ShapeDtypeStruct((B, T, aw), BF16),
        jax.ShapeDtypeStruct((B, T, aw), F32), jax.ShapeDtypeStruct((B, T, aw), BF16),
        jax.ShapeDtypeStruct((B, T, cwid), BF16), jax.ShapeDtypeStruct((B, T, cwid), BF16),
        jax.ShapeDtypeStruct((B, CONV_K - 1, cwid), F32),
    )
    return pl.pallas_call(
        _proj_p_kernel,
        out_shape=outs,
        grid=(B, T // tm),
        in_specs=[row(D), _full((1, D)), _full((D, PW)),
                  pl.BlockSpec((tm, LANES), lambda b, t: (t, 0)),
                  pl.BlockSpec((tm, LANES), lambda b, t: (t, 0)),
                  _full((1, cwid)), _full(ws.shape), _full(bs_full.shape),
                  _full(cw.shape), _full((1, cwid)), _full((1, cwid)), _full(wpw_b.shape), _full(bd.shape)],
        out_specs=(row(aw), row(aw), row(aw), row(aw), row(aw), row(aw), row(cwid), row(cwid),
                   pl.BlockSpec((None, CONV_K - 1, cwid), lambda b, t: (b, 0, 0))),
        scratch_shapes=[pltpu.VMEM((tm + 32, cwid), F32)],
        compiler_params=_cparams(("arbitrary", "arbitrary")),
        name="proj_prompt",
    )(x, g1, win_b, cos, sin, sgug, ws, bs_full, cw, cb, cg, wpw_b, bd)


def _proj_s_kernel(x_ref, g1_ref, win_ref, cos_ref, sin_ref, sgug_ref, wsx_ref, bsx_ref,
                   pre_ref, cw_ref, cb_ref, cg_ref, wpw_ref, bd_ref,
                   q_ref, kf_ref, vf_ref, osgu_ref, oconv_ref, h_ref, vn_ref):
    n = x_ref.shape[0]
    nb = pre_ref.shape[1]
    S = n // nb
    x = x_ref[...]
    hn = _rms_rows(x, g1_ref[...]).astype(BF16)
    p = jnp.dot(hn, win_ref[...], preferred_element_type=F32)
    aw = N_HEADS * HEAD_W
    q = p[:, 0:aw]
    k = p[:, aw:2 * aw]
    o = 3 * aw
    cw = osgu_ref.shape[1]
    su = p[:, o:o + cw]
    sv = p[:, o + cw:o + 2 * cw]
    ca = p[:, o + 2 * cw:o + 3 * cw]
    cg = p[:, o + 3 * cw:o + 4 * cw]
    cos = cos_ref[...]
    sin = sin_ref[...]
    lane = lax.broadcasted_iota(jnp.int32, (n, LANES), 1)
    lo_half = (lane & 32) == 0
    for h in range(N_HEADS):
        sl = slice(h * HEAD_W, (h + 1) * HEAD_W)
        q_ref[:, sl] = _rope_block(q[:, sl], cos, sin, lo_half) * 0.125
        kf_ref[:, sl] = _rope_block(k[:, sl], cos, sin, lo_half)
    vf_ref[...] = p[:, 2 * aw:3 * aw]

    bd = bd_ref[...]
    u = jax.nn.gelu(su)
    vg = jax.nn.gelu(sv)
    vn = vg * lax.rsqrt(_group_mean_sq(vg, bd, cw // SGU_GROUPS) + EPS) * sgug_ref[...]
    vn_ref[...] = vn
    for t in range(S):
        mixed = jnp.zeros((nb, cw), F32) + bsx_ref[t:t + 1, :]
        for s in range(t + 1):
            mixed = mixed + wsx_ref[t, s:s + 1, :] * vn[s * nb:(s + 1) * nb, :]
        osgu_ref[t * nb:(t + 1) * nb, :] = (u[t * nb:(t + 1) * nb, :] * mixed).astype(BF16)

    hg = ca * _sigmoid(cg)
    h_ref[...] = hg
    npre = CONV_K - 1
    ys = []
    for t in range(S):
        y = jnp.zeros((nb, cw), F32) + cb_ref[...]
        for j in range(CONV_K):
            kk = t + j
            src = pre_ref[kk] if kk < npre else hg[(kk - npre) * nb:(kk - npre + 1) * nb, :]
            y = y + src * cw_ref[j:j + 1, :]
        ys.append(y)
    y = jnp.concatenate(ys, axis=0)
    yn = y * lax.rsqrt(_group_mean_sq(y, bd, cw // CONV_GROUPS) + EPS) * cg_ref[...]
    ysw = yn * _sigmoid(yn)
    oconv_ref[...] = jnp.dot(ysw.astype(BF16), wpw_ref[...], preferred_element_type=F32).astype(BF16)


def _proj_sample(x, g1, win_b, cos, sin, sgug, wsx, bsx, prefix, cw, cb, cg, wpw_b, bd):
    n, D = x.shape
    aw = N_HEADS * HEAD_W
    cwid = sgug.shape[1]
    args = (x, g1, win_b, cos, sin, sgug, wsx, bsx, prefix, cw, cb, cg, wpw_b, bd)
    outs = (jax.ShapeDtypeStruct((n, aw), F32), jax.ShapeDtypeStruct((n, aw), F32),
            jax.ShapeDtypeStruct((n, aw), F32),
            jax.ShapeDtypeStruct((n, cwid), BF16), jax.ShapeDtypeStruct((n, cwid), BF16),
            jax.ShapeDtypeStruct((n, cwid), F32), jax.ShapeDtypeStruct((n, cwid), F32))
    return pl.pallas_call(
        _proj_s_kernel,
        out_shape=outs,
        grid=(1,),
        in_specs=[_full(a.shape) for a in args],
        out_specs=tuple(_full(o.shape) for o in outs),
        compiler_params=_cparams(("arbitrary",)),
        name="proj_sample",
    )(*args)


def _attn_p_kernel(qt_ref, kt_ref, lam4_ref, g_ref, qa_ref, qb_ref, k_ref, v_ref, o_ref,
                   m0, l0, a0, m1, l1, a1, *, lam_init):
    s_id = pl.program_id(2)
    qi = qt_ref[s_id]
    ki = kt_ref[s_id]
    tq = qa_ref.shape[0]
    tk = k_ref.shape[0]

    @pl.when(ki == 0)
    def _():
        for m, l, a in ((m0, l0, a0), (m1, l1, a1)):
            m[...] = jnp.full(m.shape, NEG_INF, F32)
            l[...] = jnp.zeros(l.shape, F32)
            a[...] = jnp.zeros(a.shape, F32)

    kk = k_ref[...]
    vv = v_ref[...]
    comps = ((qa_ref, m0, l0, a0), (qb_ref, m1, l1, a1))
    dn = (((1,), (1,)), ((), ()))
    rq = tq // ATTN_ROW_PARTS
    for part in range(ATTN_ROW_PARTS):
        rows = slice(part * rq, (part + 1) * rq)
        qpos = qi * tq + part * rq + lax.broadcasted_iota(jnp.int32, (rq, tk), 0)
        kpos = ki * tk + lax.broadcasted_iota(jnp.int32, (rq, tk), 1)
        keep = kpos <= qpos
        ss = [jnp.where(keep, lax.dot_general(q_ref[rows, :], kk, dn, preferred_element_type=F32), NEG_INF)
              for q_ref, _, _, _ in comps]
        ps = []
        for s, (_, m, l, _) in zip(ss, comps):
            m_old = m[rows, :]
            m_new = jnp.maximum(m_old, jnp.max(s, axis=-1, keepdims=True))
            alpha = jnp.exp(m_old - m_new)
            p = jnp.exp(s - m_new)
            l[rows, :] = alpha * l[rows, :] + jnp.sum(p, axis=-1, keepdims=True)
            m[rows, :] = m_new
            ps.append((p.astype(BF16), alpha))
        for (p, alpha), (_, _, _, a) in zip(ps, comps):
            a[rows, :] = alpha * a[rows, :] + jnp.dot(p, vv, preferred_element_type=F32)

    @pl.when(ki == (qi * tq + tq - 1) // tk)
    def _():
        lam = _lam_from(lam4_ref[...]) + lam_init
        o = a0[...] / l0[...] - lam * (a1[...] / l1[...])
        o_ref[...] = (_rms_rows(o, g_ref[...]) * (1.0 - lam_init)).astype(o_ref.dtype)


def _attn_prompt(qa, qb, kb, vb, lam4, g, lam_init, tq, tk):
    B, T, aw = qa.shape
    nq = T // tq
    qs, ks = [], []
    for i in range(nq):
        last = (i * tq + tq - 1) // tk
        for j in range(last + 1):
            qs.append(i)
            ks.append(j)
    qt = jnp.asarray(np.array(qs, np.int32))
    kt = jnp.asarray(np.array(ks, np.int32))
    qspec = pl.BlockSpec((None, tq, HEAD_W), lambda b, h, s, qt, kt: (b, qt[s], h))
    kspec = pl.BlockSpec((None, tk, HEAD_W), lambda b, h, s, qt, kt: (b, kt[s], h))
    grid_spec = pltpu.PrefetchScalarGridSpec(
        num_scalar_prefetch=2,
        grid=(B, N_HEADS, len(qs)),
        in_specs=[pl.BlockSpec(lam4.shape, lambda b, h, s, qt, kt: (0, 0)),
                  pl.BlockSpec(g.shape, lambda b, h, s, qt, kt: (0, 0)),
                  qspec, qspec, kspec, kspec],
        out_specs=qspec,
        scratch_shapes=[pltpu.VMEM((tq, 1), F32), pltpu.VMEM((tq, 1), F32), pltpu.VMEM((tq, HEAD_W), F32),
                        pltpu.VMEM((tq, 1), F32), pltpu.VMEM((tq, 1), F32), pltpu.VMEM((tq, HEAD_W), F32)],
    )
    return pl.pallas_call(
        functools.partial(_attn_p_kernel, lam_init=lam_init),
        out_shape=jax.ShapeDtypeStruct((B, T, aw), BF16),
        grid_spec=grid_spec,
        compiler_params=_cparams(("arbitrary", "arbitrary", "arbitrary")),
        name="attn_prompt",
    )(qt, kt, lam4, g, qa, qb, kb, vb)


def _attn_s_kernel(pt_ref, lam4_ref, g_ref, q_ref, knx_ref, vnx_ref, *rest, lam_init, ppb, n_new):
    k_refs = rest[:ppb]
    v_refs = rest[ppb:2 * ppb]
    o_ref = rest[2 * ppb]
    m_sc, l_sc, a_sc = rest[2 * ppb + 1:]
    j = pl.program_id(1)
    nj = pl.num_programs(1)

    @pl.when(j == 0)
    def _():
        m_sc[...] = jnp.full(m_sc.shape, NEG_INF, F32)
        l_sc[...] = jnp.zeros(l_sc.shape, F32)
        a_sc[...] = jnp.zeros(a_sc.shape, F32)

    q = q_ref[...]
    qb = q.astype(BF16)
    rows = q.shape[0]
    ncol = k_refs[0].shape[0]
    col = lax.broadcasted_iota(jnp.int32, (rows, ncol), 1)
    row = lax.broadcasted_iota(jnp.int32, (rows, ncol), 0)
    own = (col & (N_HEADS - 1)) == (row >> 3)
    ss = []
    for kr in k_refs:
        kk = kr[...].astype(BF16)
        s = lax.dot_general(qb, kk, (((1,), (1,)), ((), ())), preferred_element_type=F32)
        ss.append(jnp.where(own, s, NEG_INF))
    m_old = m_sc[...]
    m_new = m_old
    for s in ss:
        m_new = jnp.maximum(m_new, jnp.max(s, axis=-1, keepdims=True))
    alpha = jnp.exp(m_old - m_new)
    l = alpha * l_sc[...]
    acc = alpha * a_sc[...]
    for s, vr in zip(ss, v_refs):
        p = jnp.exp(s - m_new)
        l = l + jnp.sum(p, axis=-1, keepdims=True)
        acc = acc + jnp.dot(p.astype(BF16), vr[...].astype(BF16), preferred_element_type=F32)
    l_sc[...] = l
    a_sc[...] = acc
    m_sc[...] = m_new

    @pl.when(j == nj - 1)
    def _():
        tok = lax.broadcasted_iota(jnp.int32, (rows, 1), 0) & (n_new - 1)
        sn = []
        for c in range(n_new):
            sc = jnp.sum(q * knx_ref[c], axis=-1, keepdims=True)
            sn.append(jnp.where(tok >= c, sc, NEG_INF))
        m_old = m_sc[...]
        m_new = m_old
        for sc in sn:
            m_new = jnp.maximum(m_new, sc)
        alpha = jnp.exp(m_old - m_new)
        l = alpha * l_sc[...]
        acc = alpha * a_sc[...]
        for c, sc in enumerate(sn):
            pc = jnp.exp(sc - m_new)
            l = l + pc
            acc = acc + pc * vnx_ref[c]
        xo = acc / l
        lam = _lam_from(lam4_ref[...]) + lam_init
        o = xo - lam * pltpu.roll(xo, rows - n_new, 0)
        o_ref[...] = _rms_rows(o, g_ref[...]) * (1.0 - lam_init)


def _attn_sample(pt_flat, q32, knx, vnx, cache_k2, cache_v2, lam4, g, lam_init, layer_base, n_pages, ppb):
    nb = q32.shape[0]
    n_new = knx.shape[1]
    steps = n_pages // ppb
    prow = PAGE * N_HEADS

    def page_spec(p):
        return pl.BlockSpec((prow, HEAD_W),
                            lambda b, j, pt, p=p: (layer_base + pt[b * n_pages + j * ppb + p], 0))

    grid_spec = pltpu.PrefetchScalarGridSpec(
        num_scalar_prefetch=1,
        grid=(nb, steps),
        in_specs=[pl.BlockSpec(lam4.shape, lambda b, j, pt: (0, 0)),
                  pl.BlockSpec(g.shape, lambda b, j, pt: (0, 0)),
                  pl.BlockSpec((None, 32, HEAD_W), lambda b, j, pt: (b, 0, 0)),
                  pl.BlockSpec((None, n_new, 32, HEAD_W), lambda b, j, pt: (b, 0, 0, 0)),
                  pl.BlockSpec((None, n_new, 32, HEAD_W), lambda b, j, pt: (b, 0, 0, 0))]
                 + [page_spec(p) for p in range(ppb)] + [page_spec(p) for p in range(ppb)],
        out_specs=pl.BlockSpec((None, 32, HEAD_W), lambda b, j, pt: (b, 0, 0)),
        scratch_shapes=[pltpu.VMEM((32, 1), F32), pltpu.VMEM((32, 1), F32), pltpu.VMEM((32, HEAD_W), F32)],
    )
    return pl.pallas_call(
        functools.partial(_attn_s_kernel, lam_init=lam_init, ppb=ppb, n_new=n_new),
        out_shape=jax.ShapeDtypeStruct((nb, 32, HEAD_W), F32),
        grid_spec=grid_spec,
        compiler_params=_cparams(("arbitrary", "arbitrary")),
        name="attn_sample",
    )(pt_flat, lam4, g, q32, knx, vnx, *([cache_k2] * ppb), *([cache_v2] * ppb))


def _post_kernel(x_ref, oa_ref, os_ref, oc_ref, wout_ref, g2_ref, wq_ref, x1_ref, xn_ref, qp_ref):
    aw = oa_ref.shape[1]
    cw = os_ref.shape[1]
    x1 = (x_ref[...]
          + jnp.dot(oa_ref[...].astype(BF16), wout_ref[0:aw, :], preferred_element_type=F32)
          + jnp.dot(os_ref[...], wout_ref[aw:aw + cw, :], preferred_element_type=F32)
          + jnp.dot(oc_ref[...], wout_ref[aw + cw:aw + 2 * cw, :], preferred_element_type=F32))
    x1_ref[...] = x1
    xn = _rms_rows(x1, g2_ref[...]).astype(BF16)
    xn_ref[...] = xn
    qp = jnp.dot(xn, wq_ref[...], preferred_element_type=F32).astype(BF16)
    for h in range(PEER_HEADS):
        qp_ref[h] = qp[:, h * LANES:(h + 1) * LANES]


def _post(x, oa, osg, oc, wout_b, g2, wq_b, tm):
    n, D = x.shape
    row = lambda w: pl.BlockSpec((tm, w), lambda i: (i, 0))
    return pl.pallas_call(
        _post_kernel,
        out_shape=(jax.ShapeDtypeStruct((n, D), F32), jax.ShapeDtypeStruct((n, D), BF16),
                   jax.ShapeDtypeStruct((PEER_HEADS, n, LANES), BF16)),
        grid=(n // tm,),
        in_specs=[row(D), row(oa.shape[1]), row(osg.shape[1]), row(oc.shape[1]),
                  _full(wout_b.shape), _full((1, D)), _full(wq_b.shape)],
        out_specs=(row(D), row(D), pl.BlockSpec((PEER_HEADS, tm, LANES), lambda i: (0, i, 0))),
        compiler_params=_cparams(("arbitrary",)),
        name="post",
    )(x, oa, osg, oc, wout_b, g2, wq_b)


def _sort_pairs(n):
    pairs = []
    p = 1
    while p < n:
        k = p
        while k >= 1:
            for j in range(k % p, n - k, 2 * k):
                for i in range(min(k, n - j - k)):
                    if (i + j) // (2 * p) == (i + j + k) // (2 * p):
                        pairs.append((i + j, i + j + k))
            k //= 2
        p *= 2
    return pairs


def _cmpx(lst, i, j):
    a, b = lst[i], lst[j]
    if b is None:
        return
    if a is None:
        lst[i], lst[j] = b, None
        return
    lst[i], lst[j] = jnp.maximum(a, b), jnp.minimum(a, b)


def _sort_desc(vals):
    n = 1
    while n < len(vals):
        n *= 2
    lst = list(vals) + [None] * (n - len(vals))
    for i, j in _sort_pairs(n):
        _cmpx(lst, i, j)
    return lst


def _top16_rows(sT):
    lst = _sort_desc([sT[SUBLANES * v:SUBLANES * (v + 1), :] for v in range(N_KEYS // SUBLANES)])
    for shift in (4, 2, 1):
        other = [pltpu.roll(x, shift, 0) for x in lst]
        lst = [jnp.maximum(lst[i], other[TOPK - 1 - i]) for i in range(TOPK)]
        d = TOPK // 2
        while d >= 1:
            for i in range(TOPK):
                if (i & d) == 0:
                    _cmpx(lst, i, i + d)
            d //= 2
    return lst


def _peer_select(q, k1, k2):
    dn = (((1,), (1,)), ((), ()))
    s1 = lax.dot_general(k1, q, dn, preferred_element_type=F32)
    s2 = lax.dot_general(k2, q, dn, preferred_element_type=F32)
    a = _top16_rows(s1)
    b = _top16_rows(s2)
    cand = {}
    for r in range(TOPK):
        for c in range(TOPK // (r + 1)):
            cand[(r, c)] = a[r] + b[c]
    tau = _sort_desc(list(cand.values()))[TOPK - 1]
    m = cand[(0, 0)]
    z = jnp.zeros(tau.shape, F32)
    len_r = []
    for r in range(TOPK):
        cnt = jnp.zeros(tau.shape, F32)
        for c in range(TOPK // (r + 1)):
            sel = cand[(r, c)] >= tau
            cnt = cnt + jnp.where(sel, 1.0, 0.0)
            z = z + jnp.where(sel, jnp.exp(cand[(r, c)] - m), 0.0)
        len_r.append(cnt)
    zinv = 1.0 / z
    nv = N_KEYS // SUBLANES
    len1, e1, rank2, e2 = [], [], [], []
    for v in range(nv):
        rows = s1[SUBLANES * v:SUBLANES * (v + 1), :]
        ln = jnp.zeros(tau.shape, F32)
        for r in range(TOPK - 1, -1, -1):
            ln = jnp.where(rows >= a[r], len_r[r], ln)
        len1.append(ln)
        e1.append(jnp.exp(rows - a[0]))
        rows2 = s2[SUBLANES * v:SUBLANES * (v + 1), :]
        rk = jnp.full(tau.shape, float(TOPK), F32)
        for c in range(TOPK - 1, -1, -1):
            rk = jnp.where(rows2 >= b[c], float(c), rk)
        rank2.append(rk)
        e2.append(jnp.exp(rows2 - b[0]) * zinv)
    cat = lambda xs: jnp.concatenate(xs, axis=0)
    return cat(len1), cat(e1), cat(rank2).astype(BF16), cat(e2).astype(BF16)


def _peer_kernel(x1_ref, xn_ref, qp_ref, keys_ref, u_ref, v_ref, o_ref,
                 ln_sc, e1_sc, r2_sc, e2_sc, at_sc, ht_sc, acc_sc):
    e = pl.program_id(1)
    ne = pl.num_programs(1)
    tm = xn_ref.shape[0]
    ec = u_ref.shape[0]
    ntb = tm // LANES
    ni = ec // N_KEYS

    @pl.when(e == 0)
    def _():
        acc_sc[...] = jnp.zeros(acc_sc.shape, F32)

        def body(it, carry):
            h = it // ntb
            tb = it % ntb
            q = qp_ref[h, pl.ds(pl.multiple_of(tb * LANES, LANES), LANES), :]
            ln, e1, r2, e2 = _peer_select(q, keys_ref[2 * h], keys_ref[2 * h + 1])
            ln_sc[h, tb] = ln
            e1_sc[h, tb] = e1
            r2_sc[h, tb] = r2
            e2_sc[h, tb] = e2
            return carry

        lax.fori_loop(0, PEER_HEADS * ntb, body, 0)

    at_sc[...] = lax.dot_general(u_ref[...], xn_ref[...], (((1,), (1,)), ((), ())),
                                 preferred_element_type=F32)

    def dense(il, carry):
        ig = e * ni + il
        r0 = pl.multiple_of(il * N_KEYS, N_KEYS)
        for tb in range(ntb):
            cols = slice(tb * LANES, (tb + 1) * LANES)
            act = jax.nn.gelu(at_sc[pl.ds(r0, N_KEYS), cols].astype(BF16))
            gate = jnp.zeros((N_KEYS, LANES), BF16)
            for h in range(PEER_HEADS):
                ln = ln_sc[h, tb, pl.ds(ig, 1), :].astype(BF16)
                e1 = e1_sc[h, tb, pl.ds(ig, 1), :].astype(BF16)
                gate = gate + jnp.where(r2_sc[h, tb] < ln, e2_sc[h, tb], jnp.zeros((), BF16)) * e1
            ht_sc[pl.ds(r0, N_KEYS), cols] = act * gate
        return carry

    lax.fori_loop(0, ni, dense, 0)
    acc_sc[...] += lax.dot_general(ht_sc[...], v_ref[...], (((0,), (0,)), ((), ())),
                                   preferred_element_type=F32)

    @pl.when(e == ne - 1)
    def _():
        o_ref[...] = x1_ref[...] + acc_sc[...]


def _peer(x1, xn, qp, keys_pad, u_b, v_b, tm, ec):
    n, D = x1.shape
    ne = u_b.shape[0] // ec
    ntb = tm // LANES
    row = lambda w: pl.BlockSpec((tm, w), lambda i, e: (i, 0))
    sel32 = pltpu.VMEM((PEER_HEADS, ntb, N_KEYS, LANES), F32)
    sel16 = pltpu.VMEM((PEER_HEADS, ntb, N_KEYS, LANES), BF16)
    return pl.pallas_call(
        _peer_kernel,
        out_shape=jax.ShapeDtypeStruct((n, D), F32),
        grid=(n // tm, ne),
        in_specs=[row(D), row(D),
                  pl.BlockSpec((PEER_HEADS, tm, LANES), lambda i, e: (0, i, 0)),
                  pl.BlockSpec(keys_pad.shape, lambda i, e: (0, 0, 0)),
                  pl.BlockSpec((ec, D), lambda i, e: (e, 0)),
                  pl.BlockSpec((ec, D), lambda i, e: (e, 0))],
        out_specs=row(D),
        scratch_shapes=[sel32, sel32, sel16, sel16,
                        pltpu.VMEM((ec, tm), F32), pltpu.VMEM((ec, tm), BF16), pltpu.VMEM((tm, D), F32)],
        compiler_params=_cparams(("arbitrary", "arbitrary")),
        name="peer",
    )(x1, xn, qp, keys_pad, u_b, v_b)


def _final_kernel(x_ref, g_ref, o_ref):
    o_ref[...] = _rms_rows(x_ref[...], g_ref[...])


def _final_norm(x, g, tm):
    n, D = x.shape
    return pl.pallas_call(
        _final_kernel,
        out_shape=jax.ShapeDtypeStruct((n, D), F32),
        grid=(n // tm,),
        in_specs=[pl.BlockSpec((tm, D), lambda i: (i, 0)), _full((1, D))],
        out_specs=pl.BlockSpec((tm, D), lambda i: (i, 0)),
        compiler_params=_cparams(("arbitrary",)),
        name="final_norm",
    )(x, g)


def _rope_tables(pos):
    half = 32
    inv = ROPE_THETA ** (-jnp.arange(half, dtype=F32) / half)
    ang = pos.astype(F32)[:, None] * inv[None, :]
    c, s = jnp.cos(ang), jnp.sin(ang)
    cos = jnp.concatenate([c, c, c, c], axis=1)
    sin = jnp.concatenate([-s, s, -s, s], axis=1)
    return cos, sin


def _block_diag_ones(width, group):
    i = np.arange(width) // group
    return jnp.asarray((i[:, None] == i[None, :]).astype(np.float32), dtype=BF16)


def _pick_tile(n, pref):
    t = pref
    while n % t:
        t //= 2
    return t


def kernel(x_prompt, x_sample, cache_k, cache_v, state_conv, page_table, norm1_g, w_in, lam_q1, lam_k1,
           lam_q2, lam_k2, diff_norm_g, sgu_norm_g, w_s, b_s, conv_w, conv_b, conv_norm_g, w_conv_pw,
           w_out, norm2_g, peer_wq, peer_keys, peer_u, peer_v, final_norm_g):
    B, T, D = x_prompt.shape
    DB, S, _ = x_sample.shape
    depth = w_in.shape[0]
    n_phys = cache_k.shape[1]
    n_pages = page_table.shape[1]
    past = n_pages * PAGE
    aw = N_HEADS * HEAD_W
    cwid = sgu_norm_g.shape[1]
    gw = cwid // SGU_GROUPS
    n_exp = peer_u.shape[1]
    assert S == 4 and cwid == conv_norm_g.shape[1] and n_exp == N_KEYS * N_KEYS

    tm_p = _pick_tile(T, TM_PROJ)
    tq = _pick_tile(T, TQ_ATTN)
    tk = tq
    np_tok = B * T
    ns_tok = DB * S
    tm_post_p = _pick_tile(np_tok, TM_TOKENS)
    tm_post_s = _pick_tile(ns_tok, TM_TOKENS)
    ec = EXPERT_BLOCK
    ppb = next(p for p in (16, 8, 4, 2, 1) if n_pages % p == 0)

    cos_p, sin_p = _rope_tables(jnp.arange(T, dtype=jnp.int32))
    pos_s = past + jnp.repeat(jnp.arange(S, dtype=jnp.int32), DB)
    cos_s, sin_s = _rope_tables(pos_s)
    bd = _block_diag_ones(cwid, gw)

    cache_k2 = cache_k.reshape(depth * n_phys * PAGE * N_HEADS, HEAD_W)
    cache_v2 = cache_v.reshape(depth * n_phys * PAGE * N_HEADS, HEAD_W)
    pt_flat = page_table.reshape(-1)

    xs = jnp.transpose(x_sample, (1, 0, 2)).reshape(ns_tok, D)
    xp = x_prompt

    half = (np.arange(HEAD_W) // 64)[None, :] == np.arange(2)[:, None]
    qmask = jnp.asarray(half.astype(np.float32))[None, None, :, None, :]

    kp_l, vp_l, ks_l, vs_l, cp_l, cs_l, gs_l = [], [], [], [], [], [], []
    for l in range(depth):
        lam_init = 0.8 - 0.6 * math.exp(-0.3 * l)
        win_b = w_in[l].astype(BF16)
        wout_b = w_out[l].astype(BF16)
        wq_b = peer_wq[l].astype(BF16)
        wpw_b = w_conv_pw[l].astype(BF16)
        u_b = peer_u[l].astype(BF16)
        v_b = peer_v[l].astype(BF16)
        hw = peer_keys.shape[-1]
        kz = jnp.zeros(peer_keys.shape[1:4] + (LANES - hw,), F32)
        keys_pad = jnp.stack([jnp.concatenate([peer_keys[l, :, 0], kz[:, 0]], axis=-1),
                              jnp.concatenate([kz[:, 1], peer_keys[l, :, 1]], axis=-1)], axis=1)
        keys_pad = keys_pad.reshape(2 * PEER_HEADS, N_KEYS, LANES).astype(BF16)
        lam4 = jnp.stack([lam_q1[l], lam_k1[l], lam_q2[l], lam_k2[l]])
        g1 = norm1_g[l][None, :]
        g2 = norm2_g[l][None, :]
        gd = diff_norm_g[l][None, :]
        sgug = sgu_norm_g[l][None, :]
        cb = conv_b[l][None, :]
        cg = conv_norm_g[l][None, :]
        bs_full = jnp.repeat(b_s[l].T, gw, axis=1)
        wsx = jnp.repeat(jnp.transpose(w_s[l][:, :S, :S], (1, 2, 0)), gw, axis=2)
        bsx = bs_full[:S]

        qa, qb, kf, kb, vf, vb, osg, oc, crow = _proj_prompt(
            xp, g1, win_b, cos_p, sin_p, sgug, w_s[l], bs_full, conv_w[l], cb, cg, wpw_b, bd, tm_p)
        oa = _attn_prompt(qa, qb, kb, vb, lam4, gd, lam_init, tq, tk)
        x1, xn, qp = _post(xp.reshape(np_tok, D), oa.reshape(np_tok, aw), osg.reshape(np_tok, cwid),
                           oc.reshape(np_tok, cwid), wout_b, g2, wq_b, tm_post_p)
        xp = _peer(x1, xn, qp, keys_pad, u_b, v_b, tm_post_p, ec).reshape(B, T, D)
        kp_l.append(kf.reshape(B, T, N_HEADS, HEAD_W))
        vp_l.append(vf.reshape(B, T, N_HEADS, HEAD_W))
        cp_l.append(crow)

        prefix = jnp.transpose(state_conv[l], (1, 0, 2))
        qs, ksf, vsf, osg_s, oc_s, h_s, vn_s = _proj_sample(
            xs, g1, win_b, cos_s, sin_s, sgug, wsx, bsx, prefix, conv_w[l], cb, cg, wpw_b, bd)
        to_b = lambda a: jnp.transpose(a.reshape(S, DB, a.shape[-1]), (1, 0, 2))
        q_b, k_b, v_b2 = to_b(qs), to_b(ksf), to_b(vsf)
        heads = lambda a: jnp.transpose(a.reshape(DB, S, N_HEADS, HEAD_W), (0, 2, 1, 3))
        q32 = (heads(q_b)[:, :, None] * qmask).reshape(DB, 2 * N_HEADS * S, HEAD_W)
        rep = lambda a: jnp.repeat(a.reshape(DB, S, N_HEADS, HEAD_W), 2 * S, axis=2)
        oa_s32 = _attn_sample(pt_flat, q32, rep(k_b), rep(v_b2), cache_k2, cache_v2, lam4, gd, lam_init,
                              l * n_phys, n_pages, ppb)
        oa_s = oa_s32.reshape(DB, N_HEADS, 2, S, HEAD_W)[:, :, 0]
        oa_s = jnp.transpose(oa_s, (2, 0, 1, 3)).reshape(ns_tok, aw)
        x1s, xns, qps = _post(xs, oa_s, osg_s, oc_s, wout_b, g2, wq_b, tm_post_s)
        xs = _peer(x1s, xns, qps, keys_pad, u_b, v_b, tm_post_s, ec)
        ks_l.append(k_b.reshape(DB, S, N_HEADS, HEAD_W))
        vs_l.append(v_b2.reshape(DB, S, N_HEADS, HEAD_W))
        cs_l.append(jnp.concatenate([state_conv[l][:, S:], to_b(h_s)], axis=1))
        gs_l.append(to_b(vn_s))

    fg = final_norm_g[None, :]
    y_prompt = _final_norm(xp.reshape(np_tok, D), fg, tm_post_p).reshape(B, T, D)
    y_sample = jnp.transpose(_final_norm(xs, fg, tm_post_s).reshape(S, DB, D), (1, 0, 2))
    return (y_prompt, y_sample, jnp.stack(kp_l), jnp.stack(vp_l), jnp.stack(ks_l), jnp.stack(vs_l),
            jnp.stack(cp_l), jnp.stack(cs_l), jnp.stack(gs_l))
```

```python
import functools
import math

import numpy as np
import jax
import jax.numpy as jnp
from jax import lax
from jax.experimental import pallas as pl
from jax.experimental.pallas import tpu as pltpu

F32 = jnp.float32
BF16 = jnp.bfloat16

LANES = 128
SUBLANES = 8
EPS = 1e-6
NEG_INF = -1e30
ROPE_THETA = 10000.0

N_HEADS = 4
HEAD_W = 128
SGU_GROUPS = 4
CONV_GROUPS = 4
CONV_K = 31
CHUNK = 128
PAGE = 128
PEER_HEADS = 8
N_KEYS = 128
TOPK = 16

VMEM_LIMIT = 56 * 1024 * 1024

TM_PROJ = 512
TQ_ATTN = 1024
ATTN_ROW_PARTS = 4
TM_TOKENS = 512
EXPERT_BLOCK = 2048
PEER_STREAMS = 4


def _cparams(sem):
    return pltpu.CompilerParams(dimension_semantics=sem, vmem_limit_bytes=VMEM_LIMIT)


def _full(shape):
    n = len(shape)
    return pl.BlockSpec(shape, lambda *_: (0,) * n)


def _sigmoid(x):
    return 1.0 / (1.0 + jnp.exp(-x))


def _rms_rows(x, g):
    return x * lax.rsqrt(jnp.mean(x * x, axis=-1, keepdims=True) + EPS) * g


def _group_mean_sq(x, bd, group):
    x2 = x * x
    hi = x2.astype(BF16)
    lo = (x2 - hi.astype(F32)).astype(BF16)
    s = jnp.dot(hi, bd, preferred_element_type=F32) + jnp.dot(lo, bd, preferred_element_type=F32)
    return s * (1.0 / group)


def _rope_block(xh, cos, sin, lo_half):
    sw = jnp.where(lo_half, pltpu.roll(xh, LANES - 32, 1), pltpu.roll(xh, 32, 1))
    return xh * cos + sw * sin


def _lam_from(lam4):
    a = jnp.sum(lam4[0:1, :] * lam4[1:2, :], axis=-1, keepdims=True)
    b = jnp.sum(lam4[2:3, :] * lam4[3:4, :], axis=-1, keepdims=True)
    return jnp.exp(a) - jnp.exp(b)


def _proj_p_kernel(x_ref, g1_ref, win_ref, cos_ref, sin_ref, sgug_ref, ws_ref, bs_ref,
                   cw_ref, cb_ref, cg_ref, wpw_ref, bd_ref,
                   qa_ref, qb_ref, kf_ref, kb_ref, vf_ref, vb_ref, osgu_ref, oconv_ref, crow_ref,
                   hbuf):
    t = pl.program_id(1)
    tm = x_ref.shape[0]
    x = x_ref[...]
    hn = _rms_rows(x, g1_ref[...]).astype(BF16)
    p = jnp.dot(hn, win_ref[...], preferred_element_type=F32)
    aw = N_HEADS * HEAD_W
    q = p[:, 0:aw]
    k = p[:, aw:2 * aw]
    v = p[:, 2 * aw:3 * aw]
    o = 3 * aw
    cw = osgu_ref.shape[1]
    su = p[:, o:o + cw]
    sv = p[:, o + cw:o + 2 * cw]
    ca = p[:, o + 2 * cw:o + 3 * cw]
    cg = p[:, o + 3 * cw:o + 4 * cw]

    cos = cos_ref[...]
    sin = sin_ref[...]
    lane = lax.broadcasted_iota(jnp.int32, (tm, LANES), 1)
    lo_half = (lane & 32) == 0
    first = (lane & 64) == 0
    for h in range(N_HEADS):
        sl = slice(h * HEAD_W, (h + 1) * HEAD_W)
        qr = _rope_block(q[:, sl], cos, sin, lo_half) * 0.125
        qa_ref[:, sl] = jnp.where(first, qr, 0.0).astype(BF16)
        qb_ref[:, sl] = jnp.where(first, 0.0, qr).astype(BF16)
        kr = _rope_block(k[:, sl], cos, sin, lo_half)
        kf_ref[:, sl] = kr
        kb_ref[:, sl] = kr.astype(BF16)
    vf_ref[...] = v
    vb_ref[...] = v.astype(BF16)

    bd = bd_ref[...]
    u = jax.nn.gelu(su)
    vg = jax.nn.gelu(sv)
    vn = vg * lax.rsqrt(_group_mean_sq(vg, bd, cw // SGU_GROUPS) + EPS) * sgug_ref[...]
    vnb = vn.astype(BF16)
    r_i = lax.broadcasted_iota(jnp.int32, (CHUNK, CHUNK), 0)
    c_i = lax.broadcasted_iota(jnp.int32, (CHUNK, CHUNK), 1)
    tril = r_i >= c_i
    ws = [jnp.where(tril, ws_ref[g], 0.0).astype(BF16) for g in range(SGU_GROUPS)]
    left = c_i < 64
    bias = bs_ref[...]
    for c in range(tm // CHUNK):
        rows = slice(c * CHUNK, (c + 1) * CHUNK)
        parts = []
        for gp in range(SGU_GROUPS // 2):
            r = vnb[rows, gp * LANES:(gp + 1) * LANES]
            parts.append(jnp.where(left,
                                   jnp.dot(ws[2 * gp], r, preferred_element_type=F32),
                                   jnp.dot(ws[2 * gp + 1], r, preferred_element_type=F32)))
        mixed = jnp.concatenate(parts, axis=1) + bias
        osgu_ref[rows, :] = (u[rows, :] * mixed).astype(BF16)

    hg = ca * _sigmoid(cg)

    @pl.when(t == 0)
    def _():
        hbuf[0:32, :] = jnp.zeros((32, cw), F32)

    hbuf[32:32 + tm, :] = hg
    y = jnp.zeros((tm, cw), F32) + cb_ref[...]
    for j in range(CONV_K):
        y = y + hbuf[pl.ds(2 + j, tm), :] * cw_ref[j:j + 1, :]
    yn = y * lax.rsqrt(_group_mean_sq(y, bd, cw // CONV_GROUPS) + EPS) * cg_ref[...]
    ys = yn * _sigmoid(yn)
    oconv_ref[...] = jnp.dot(ys.astype(BF16), wpw_ref[...], preferred_element_type=F32).astype(BF16)
    crow_ref[...] = hbuf[pl.ds(tm + 2, CONV_K - 1), :]
    hbuf[0:32, :] = hbuf[tm:tm + 32, :]


def _proj_prompt(x, g1, win_b, cos, sin, sgug, ws, bs_full, cw, cb, cg, wpw_b, bd, tm):
    B, T, D = x.shape
    PW = win_b.shape[1]
    aw = N_HEADS * HEAD_W
    cwid = sgug.shape[1]
    row = lambda w: pl.BlockSpec((None, tm, w), lambda b, t: (b, t, 0))
    outs = (
        jax.ShapeDtypeStruct((B, T, aw), BF16), jax.ShapeDtypeStruct((B, T, aw), BF16),
        jax.ShapeDtypeStruct((B, T, aw), F32), jax.ShapeDtypeStruct((B, T, aw), BF16),
        jax.ShapeDtypeStruct((B, T, aw), F32), jax.ShapeDtypeStruct((B, T, aw), BF16),
        jax.ShapeDtypeStruct((B, T, cwid), BF16), jax.ShapeDtypeStruct((B, T, cwid), BF16),
        jax.ShapeDtypeStruct((B, CONV_K - 1, cwid), F32),
    )
    return pl.pallas_call(
        _proj_p_kernel,
        out_shape=outs,
        grid=(B, T // tm),
        in_specs=[row(D), _full((1, D)), _full((D, PW)),
                  pl.BlockSpec((tm, LANES), lambda b, t: (t, 0)),
                  pl.BlockSpec((tm, LANES), lambda b, t: (t, 0)),
                  _full((1, cwid)), _full(ws.shape), _full(bs_full.shape),
                  _full(cw.shape), _full((1, cwid)), _full((1, cwid)), _full(wpw_b.shape), _full(bd.shape)],
        out_specs=(row(aw), row(aw), row(aw), row(aw), row(aw), row(aw), row(cwid), row(cwid),
                   pl.BlockSpec((None, CONV_K - 1, cwid), lambda b, t: (b, 0, 0))),
        scratch_shapes=[pltpu.VMEM((tm + 32, cwid), F32)],
        compiler_params=_cparams(("arbitrary", "arbitrary")),
        name="proj_prompt",
    )(x, g1, win_b, cos, sin, sgug, ws, bs_full, cw, cb, cg, wpw_b, bd)


def _proj_s_kernel(x_ref, g1_ref, win_ref, cos_ref, sin_ref, sgug_ref, wsx_ref, bsx_ref,
                   pre_ref, cw_ref, cb_ref, cg_ref, wpw_ref, bd_ref,
                   q_ref, kf_ref, vf_ref, osgu_ref, oconv_ref, h_ref, vn_ref):
    n = x_ref.shape[0]
    nb = pre_ref.shape[1]
    S = n // nb
    x = x_ref[...]
    hn = _rms_rows(x, g1_ref[...]).astype(BF16)
    p = jnp.dot(hn, win_ref[...], preferred_element_type=F32)
    aw = N_HEADS * HEAD_W
    q = p[:, 0:aw]
    k = p[:, aw:2 * aw]
    o = 3 * aw
    cw = osgu_ref.shape[1]
    su = p[:, o:o + cw]
    sv = p[:, o + cw:o + 2 * cw]
    ca = p[:, o + 2 * cw:o + 3 * cw]
    cg = p[:, o + 3 * cw:o + 4 * cw]
    cos = cos_ref[...]
    sin = sin_ref[...]
    lane = lax.broadcasted_iota(jnp.int32, (n, LANES), 1)
    lo_half = (lane & 32) == 0
    for h in range(N_HEADS):
        sl = slice(h * HEAD_W, (h + 1) * HEAD_W)
        q_ref[:, sl] = _rope_block(q[:, sl], cos, sin, lo_half) * 0.125
        kf_ref[:, sl] = _rope_block(k[:, sl], cos, sin, lo_half)
    vf_ref[...] = p[:, 2 * aw:3 * aw]

    bd = bd_ref[...]
    u = jax.nn.gelu(su)
    vg = jax.nn.gelu(sv)
    vn = vg * lax.rsqrt(_group_mean_sq(vg, bd, cw // SGU_GROUPS) + EPS) * sgug_ref[...]
    vn_ref[...] = vn
    for t in range(S):
        mixed = jnp.zeros((nb, cw), F32) + bsx_ref[t:t + 1, :]
        for s in range(t + 1):
            mixed = mixed + wsx_ref[t, s:s + 1, :] * vn[s * nb:(s + 1) * nb, :]
        osgu_ref[t * nb:(t + 1) * nb, :] = (u[t * nb:(t + 1) * nb, :] * mixed).astype(BF16)

    hg = ca * _sigmoid(cg)
    h_ref[...] = hg
    npre = CONV_K - 1
    ys = []
    for t in range(S):
        y = jnp.zeros((nb, cw), F32) + cb_ref[...]
        for j in range(CONV_K):
            kk = t + j
            src = pre_ref[kk] if kk < npre else hg[(kk - npre) * nb:(kk - npre + 1) * nb, :]
            y = y + src * cw_ref[j:j + 1, :]
        ys.append(y)
    y = jnp.concatenate(ys, axis=0)
    yn = y * lax.rsqrt(_group_mean_sq(y, bd, cw // CONV_GROUPS) + EPS) * cg_ref[...]
    ysw = yn * _sigmoid(yn)
    oconv_ref[...] = jnp.dot(ysw.astype(BF16), wpw_ref[...], preferred_element_type=F32).astype(BF16)


def _proj_sample(x, g1, win_b, cos, sin, sgug, wsx, bsx, prefix, cw, cb, cg, wpw_b, bd):
    n, D = x.shape
    aw = N_HEADS * HEAD_W
    cwid = sgug.shape[1]
    args = (x, g1, win_b, cos, sin, sgug, wsx, bsx, prefix, cw, cb, cg, wpw_b, bd)
    outs = (jax.ShapeDtypeStruct((n, aw), F32), jax.ShapeDtypeStruct((n, aw), F32),
            jax.ShapeDtypeStruct((n, aw), F32),
            jax.ShapeDtypeStruct((n, cwid), BF16), jax.ShapeDtypeStruct((n, cwid), BF16),
            jax.ShapeDtypeStruct((n, cwid), F32), jax.ShapeDtypeStruct((n, cwid), F32))
    return pl.pallas_call(
        _proj_s_kernel,
        out_shape=outs,
        grid=(1,),
        in_specs=[_full(a.shape) for a in args],
        out_specs=tuple(_full(o.shape) for o in outs),
        compiler_params=_cparams(("arbitrary",)),
        name="proj_sample",
    )(*args)


def _attn_p_kernel(qt_ref, kt_ref, lam4_ref, g_ref, qa_ref, qb_ref, k_ref, v_ref, o_ref,
                   m0, l0, a0, m1, l1, a1, *, lam_init):
    s_id = pl.program_id(2)
    qi = qt_ref[s_id]
    ki = kt_ref[s_id]
    tq = qa_ref.shape[0]
    tk = k_ref.shape[0]

    @pl.when(ki == 0)
    def _():
        for m, l, a in ((m0, l0, a0), (m1, l1, a1)):
            m[...] = jnp.full(m.shape, NEG_INF, F32)
            l[...] = jnp.zeros(l.shape, F32)
            a[...] = jnp.zeros(a.shape, F32)

    kk = k_ref[...]
    vv = v_ref[...]
    comps = ((qa_ref, m0, l0, a0), (qb_ref, m1, l1, a1))
    dn = (((1,), (1,)), ((), ()))
    rq = tq // ATTN_ROW_PARTS
    for part in range(ATTN_ROW_PARTS):
        rows = slice(part * rq, (part + 1) * rq)
        qpos = qi * tq + part * rq + lax.broadcasted_iota(jnp.int32, (rq, tk), 0)
        kpos = ki * tk + lax.broadcasted_iota(jnp.int32, (rq, tk), 1)
        keep = kpos <= qpos
        ss = [jnp.where(keep, lax.dot_general(q_ref[rows, :], kk, dn, preferred_element_type=F32), NEG_INF)
              for q_ref, _, _, _ in comps]
        ps = []
        for s, (_, m, l, _) in zip(ss, comps):
            m_old = m[rows, :]
            m_new = jnp.maximum(m_old, jnp.max(s, axis=-1, keepdims=True))
            alpha = jnp.exp(m_old - m_new)
            p = jnp.exp(s - m_new)
            l[rows, :] = alpha * l[rows, :] + jnp.sum(p, axis=-1, keepdims=True)
            m[rows, :] = m_new
            ps.append((p.astype(BF16), alpha))
        for (p, alpha), (_, _, _, a) in zip(ps, comps):
            a[rows, :] = alpha * a[rows, :] + jnp.dot(p, vv, preferred_element_type=F32)

    @pl.when(ki == (qi * tq + tq - 1) // tk)
    def _():
        lam = _lam_from(lam4_ref[...]) + lam_init
        o = a0[...] / l0[...] - lam * (a1[...] / l1[...])
        o_ref[...] = (_rms_rows(o, g_ref[...]) * (1.0 - lam_init)).astype(o_ref.dtype)


def _attn_prompt(qa, qb, kb, vb, lam4, g, lam_init, tq, tk):
    B, T, aw = qa.shape
    nq = T // tq
    qs, ks = [], []
    for i in range(nq):
        last = (i * tq + tq - 1) // tk
        for j in range(last + 1):
            qs.append(i)
            ks.append(j)
    qt = jnp.asarray(np.array(qs, np.int32))
    kt = jnp.asarray(np.array(ks, np.int32))
    qspec = pl.BlockSpec((None, tq, HEAD_W), lambda b, h, s, qt, kt: (b, qt[s], h))
    kspec = pl.BlockSpec((None, tk, HEAD_W), lambda b, h, s, qt, kt: (b, kt[s], h))
    grid_spec = pltpu.PrefetchScalarGridSpec(
        num_scalar_prefetch=2,
        grid=(B, N_HEADS, len(qs)),
        in_specs=[pl.BlockSpec(lam4.shape, lambda b, h, s, qt, kt: (0, 0)),
                  pl.BlockSpec(g.shape, lambda b, h, s, qt, kt: (0, 0)),
                  qspec, qspec, kspec, kspec],
        out_specs=qspec,
        scratch_shapes=[pltpu.VMEM((tq, 1), F32), pltpu.VMEM((tq, 1), F32), pltpu.VMEM((tq, HEAD_W), F32),
                        pltpu.VMEM((tq, 1), F32), pltpu.VMEM((tq, 1), F32), pltpu.VMEM((tq, HEAD_W), F32)],
    )
    return pl.pallas_call(
        functools.partial(_attn_p_kernel, lam_init=lam_init),
        out_shape=jax.ShapeDtypeStruct((B, T, aw), BF16),
        grid_spec=grid_spec,
        compiler_params=_cparams(("arbitrary", "arbitrary", "arbitrary")),
        name="attn_prompt",
    )(qt, kt, lam4, g, qa, qb, kb, vb)


def _attn_s_kernel(pt_ref, lam4_ref, g_ref, q_ref, knx_ref, vnx_ref, *rest, lam_init, ppb, n_new):
    k_refs = rest[:ppb]
    v_refs = rest[ppb:2 * ppb]
    o_ref = rest[2 * ppb]
    m_sc, l_sc, a_sc = rest[2 * ppb + 1:]
    j = pl.program_id(1)
    nj = pl.num_programs(1)

    @pl.when(j == 0)
    def _():
        m_sc[...] = jnp.full(m_sc.shape, NEG_INF, F32)
        l_sc[...] = jnp.zeros(l_sc.shape, F32)
        a_sc[...] = jnp.zeros(a_sc.shape, F32)

    q = q_ref[...]
    qb = q.astype(BF16)
    rows = q.shape[0]
    ncol = k_refs[0].shape[0]
    col = lax.broadcasted_iota(jnp.int32, (rows, ncol), 1)
    row = lax.broadcasted_iota(jnp.int32, (rows, ncol), 0)
    own = (col & (N_HEADS - 1)) == (row >> 3)
    ss = []
    for kr in k_refs:
        kk = kr[...].astype(BF16)
        s = lax.dot_general(qb, kk, (((1,), (1,)), ((), ())), preferred_element_type=F32)
        ss.append(jnp.where(own, s, NEG_INF))
    m_old = m_sc[...]
    m_new = m_old
    for s in ss:
        m_new = jnp.maximum(m_new, jnp.max(s, axis=-1, keepdims=True))
    alpha = jnp.exp(m_old - m_new)
    l = alpha * l_sc[...]
    acc = alpha * a_sc[...]
    for s, vr in zip(ss, v_refs):
        p = jnp.exp(s - m_new)
        l = l + jnp.sum(p, axis=-1, keepdims=True)
        acc = acc + jnp.dot(p.astype(BF16), vr[...].astype(BF16), preferred_element_type=F32)
    l_sc[...] = l
    a_sc[...] = acc
    m_sc[...] = m_new

    @pl.when(j == nj - 1)
    def _():
        tok = lax.broadcasted_iota(jnp.int32, (rows, 1), 0) & (n_new - 1)
        sn = []
        for c in range(n_new):
            sc = jnp.sum(q * knx_ref[c], axis=-1, keepdims=True)
            sn.append(jnp.where(tok >= c, sc, NEG_INF))
        m_old = m_sc[...]
        m_new = m_old
        for sc in sn:
            m_new = jnp.maximum(m_new, sc)
        alpha = jnp.exp(m_old - m_new)
        l = alpha * l_sc[...]
        acc = alpha * a_sc[...]
        for c, sc in enumerate(sn):
            pc = jnp.exp(sc - m_new)
            l = l + pc
            acc = acc + pc * vnx_ref[c]
        xo = acc / l
        lam = _lam_from(lam4_ref[...]) + lam_init
        o = xo - lam * pltpu.roll(xo, rows - n_new, 0)
        o_ref[...] = _rms_rows(o, g_ref[...]) * (1.0 - lam_init)


def _attn_sample(pt_flat, q32, knx, vnx, cache_k2, cache_v2, lam4, g, lam_init, layer_base, n_pages, ppb):
    nb = q32.shape[0]
    n_new = knx.shape[1]
    steps = n_pages // ppb
    prow = PAGE * N_HEADS

    def page_spec(p):
        return pl.BlockSpec((prow, HEAD_W),
                            lambda b, j, pt, p=p: (layer_base + pt[b * n_pages + j * ppb + p], 0))

    grid_spec = pltpu.PrefetchScalarGridSpec(
        num_scalar_prefetch=1,
        grid=(nb, steps),
        in_specs=[pl.BlockSpec(lam4.shape, lambda b, j, pt: (0, 0)),
                  pl.BlockSpec(g.shape, lambda b, j, pt: (0, 0)),
                  pl.BlockSpec((None, 32, HEAD_W), lambda b, j, pt: (b, 0, 0)),
                  pl.BlockSpec((None, n_new, 32, HEAD_W), lambda b, j, pt: (b, 0, 0, 0)),
                  pl.BlockSpec((None, n_new, 32, HEAD_W), lambda b, j, pt: (b, 0, 0, 0))]
                 + [page_spec(p) for p in range(ppb)] + [page_spec(p) for p in range(ppb)],
        out_specs=pl.BlockSpec((None, 32, HEAD_W), lambda b, j, pt: (b, 0, 0)),
        scratch_shapes=[pltpu.VMEM((32, 1), F32), pltpu.VMEM((32, 1), F32), pltpu.VMEM((32, HEAD_W), F32)],
    )
    return pl.pallas_call(
        functools.partial(_attn_s_kernel, lam_init=lam_init, ppb=ppb, n_new=n_new),
        out_shape=jax.ShapeDtypeStruct((nb, 32, HEAD_W), F32),
        grid_spec=grid_spec,
        compiler_params=_cparams(("arbitrary", "arbitrary")),
        name="attn_sample",
    )(pt_flat, lam4, g, q32, knx, vnx, *([cache_k2] * ppb), *([cache_v2] * ppb))


def _post_kernel(x_ref, oa_ref, os_ref, oc_ref, wout_ref, g2_ref, wq_ref, x1_ref, xn_ref, qp_ref):
    aw = oa_ref.shape[1]
    cw = os_ref.shape[1]
    x1 = (x_ref[...]
          + jnp.dot(oa_ref[...].astype(BF16), wout_ref[0:aw, :], preferred_element_type=F32)
          + jnp.dot(os_ref[...], wout_ref[aw:aw + cw, :], preferred_element_type=F32)
          + jnp.dot(oc_ref[...], wout_ref[aw + cw:aw + 2 * cw, :], preferred_element_type=F32))
    x1_ref[...] = x1
    xn = _rms_rows(x1, g2_ref[...]).astype(BF16)
    xn_ref[...] = xn
    qp = jnp.dot(xn, wq_ref[...], preferred_element_type=F32).astype(BF16)
    for h in range(PEER_HEADS):
        qp_ref[h] = qp[:, h * LANES:(h + 1) * LANES]


def _post(x, oa, osg, oc, wout_b, g2, wq_b, tm):
    n, D = x.shape
    row = lambda w: pl.BlockSpec((tm, w), lambda i: (i, 0))
    return pl.pallas_call(
        _post_kernel,
        out_shape=(jax.ShapeDtypeStruct((n, D), F32), jax.ShapeDtypeStruct((n, D), BF16),
                   jax.ShapeDtypeStruct((PEER_HEADS, n, LANES), BF16)),
        grid=(n // tm,),
        in_specs=[row(D), row(oa.shape[1]), row(osg.shape[1]), row(oc.shape[1]),
                  _full(wout_b.shape), _full((1, D)), _full(wq_b.shape)],
        out_specs=(row(D), row(D), pl.BlockSpec((PEER_HEADS, tm, LANES), lambda i: (0, i, 0))),
        compiler_params=_cparams(("arbitrary",)),
        name="post",
    )(x, oa, osg, oc, wout_b, g2, wq_b)


def _sort_pairs(n):
    pairs = []
    p = 1
    while p < n:
        k = p
        while k >= 1:
            for j in range(k % p, n - k, 2 * k):
                for i in range(min(k, n - j - k)):
                    if (i + j) // (2 * p) == (i + j + k) // (2 * p):
                        pairs.append((i + j, i + j + k))
            k //= 2
        p *= 2
    return pairs


def _cmpx(lst, i, j):
    a, b = lst[i], lst[j]
    if b is None:
        return
    if a is None:
        lst[i], lst[j] = b, None
        return
    lst[i], lst[j] = jnp.maximum(a, b), jnp.minimum(a, b)


def _sort_desc(vals):
    n = 1
    while n < len(vals):
        n *= 2
    lst = list(vals) + [None] * (n - len(vals))
    for i, j in _sort_pairs(n):
        _cmpx(lst, i, j)
    return lst


def _top16_rows(sT):
    lst = _sort_desc([sT[SUBLANES * v:SUBLANES * (v + 1), :] for v in range(N_KEYS // SUBLANES)])
    for shift in (4, 2, 1):
        other = [pltpu.roll(x, shift, 0) for x in lst]
        lst = [jnp.maximum(lst[i], other[TOPK - 1 - i]) for i in range(TOPK)]
        d = TOPK // 2
        while d >= 1:
            for i in range(TOPK):
                if (i & d) == 0:
                    _cmpx(lst, i, i + d)
            d //= 2
    return lst


def _peer_select(q, k1, k2):
    dn = (((1,), (1,)), ((), ()))
    s1 = lax.dot_general(k1, q, dn, preferred_element_type=F32)
    s2 = lax.dot_general(k2, q, dn, preferred_element_type=F32)
    a = _top16_rows(s1)
    b = _top16_rows(s2)
    cand = {}
    for r in range(TOPK):
        for c in range(TOPK // (r + 1)):
            cand[(r, c)] = a[r] + b[c]
    tau = _sort_desc(list(cand.values()))[TOPK - 1]
    m = cand[(0, 0)]
    z = jnp.zeros(tau.shape, F32)
    len_r = []
    for r in range(TOPK):
        cnt = jnp.zeros(tau.shape, F32)
        for c in range(TOPK // (r + 1)):
            sel = cand[(r, c)] >= tau
            cnt = cnt + jnp.where(sel, 1.0, 0.0)
            z = z + jnp.where(sel, jnp.exp(cand[(r, c)] - m), 0.0)
        len_r.append(cnt)
    zinv = 1.0 / z
    nv = N_KEYS // SUBLANES
    len1, e1, rank2, e2 = [], [], [], []
    for v in range(nv):
        rows = s1[SUBLANES * v:SUBLANES * (v + 1), :]
        ln = jnp.zeros(tau.shape, F32)
        for r in range(TOPK - 1, -1, -1):
            ln = jnp.where(rows >= a[r], len_r[r], ln)
        len1.append(ln)
        e1.append(jnp.exp(rows - a[0]))
        rows2 = s2[SUBLANES * v:SUBLANES * (v + 1), :]
        rk = jnp.full(tau.shape, float(TOPK), F32)
        for c in range(TOPK - 1, -1, -1):
            rk = jnp.where(rows2 >= b[c], float(c), rk)
        rank2.append(rk)
        e2.append(jnp.exp(rows2 - b[0]) * zinv)
    cat = lambda xs: jnp.concatenate(xs, axis=0)
    return cat(len1), cat(e1), cat(rank2).astype(BF16), cat(e2).astype(BF16)


def _peer_kernel(x1_ref, xn_ref, qp_ref, keys_ref, *rest, nstream):
    u_refs = rest[:nstream]
    v_refs = rest[nstream:2 * nstream]
    o_ref, ln_sc, e1_sc, r2_sc, e2_sc, at_sc, ht_sc, acc_sc = rest[2 * nstream:]
    e = pl.program_id(1)
    ne = pl.num_programs(1)
    tm = xn_ref.shape[0]
    rows_s = u_refs[0].shape[0]
    ec = rows_s * nstream
    ntb = tm // LANES
    ni = ec // N_KEYS

    @pl.when(e == 0)
    def _():
        acc_sc[...] = jnp.zeros(acc_sc.shape, F32)

        def body(it, carry):
            h = it // ntb
            tb = it % ntb
            q = qp_ref[h, pl.ds(pl.multiple_of(tb * LANES, LANES), LANES), :]
            ln, e1, r2, e2 = _peer_select(q, keys_ref[2 * h], keys_ref[2 * h + 1])
            ln_sc[h, tb] = ln
            e1_sc[h, tb] = e1
            r2_sc[h, tb] = r2
            e2_sc[h, tb] = e2
            return carry

        lax.fori_loop(0, PEER_HEADS * ntb, body, 0)

    for k, u_ref in enumerate(u_refs):
        at_sc[k * rows_s:(k + 1) * rows_s, :] = lax.dot_general(
            u_ref[...], xn_ref[...], (((1,), (1,)), ((), ())), preferred_element_type=F32)

    def dense(il, carry):
        ig = e * ni + il
        r0 = pl.multiple_of(il * N_KEYS, N_KEYS)
        for tb in range(ntb):
            cols = slice(tb * LANES, (tb + 1) * LANES)
            act = jax.nn.gelu(at_sc[pl.ds(r0, N_KEYS), cols].astype(BF16))
            gate = jnp.zeros((N_KEYS, LANES), BF16)
            for h in range(PEER_HEADS):
                ln = ln_sc[h, tb, pl.ds(ig, 1), :].astype(BF16)
                e1 = e1_sc[h, tb, pl.ds(ig, 1), :].astype(BF16)
                gate = gate + jnp.where(r2_sc[h, tb] < ln, e2_sc[h, tb], jnp.zeros((), BF16)) * e1
            ht_sc[pl.ds(r0, N_KEYS), cols] = act * gate
        return carry

    lax.fori_loop(0, ni, dense, 0)
    part = None
    for k, v_ref in enumerate(v_refs):
        d = lax.dot_general(ht_sc[k * rows_s:(k + 1) * rows_s, :], v_ref[...], (((0,), (0,)), ((), ())),
                            preferred_element_type=F32)
        part = d if part is None else part + d
    acc_sc[...] += part

    @pl.when(e == ne - 1)
    def _():
        o_ref[...] = x1_ref[...] + acc_sc[...]


def _peer(x1, xn, qp, keys_pad, u_b, v_b, tm, ec):
    n, D = x1.shape
    ne = u_b.shape[0] // ec
    ntb = tm // LANES
    row = lambda w: pl.BlockSpec((tm, w), lambda i, e: (i, 0))
    sel32 = pltpu.VMEM((PEER_HEADS, ntb, N_KEYS, LANES), F32)
    sel16 = pltpu.VMEM((PEER_HEADS, ntb, N_KEYS, LANES), BF16)
    ns = PEER_STREAMS
    piece = [pl.BlockSpec((ec // ns, D), lambda i, e, k=k: (ns * e + k, 0)) for k in range(ns)]
    return pl.pallas_call(
        functools.partial(_peer_kernel, nstream=ns),
        out_shape=jax.ShapeDtypeStruct((n, D), F32),
        grid=(n // tm, ne),
        in_specs=[row(D), row(D),
                  pl.BlockSpec((PEER_HEADS, tm, LANES), lambda i, e: (0, i, 0)),
                  pl.BlockSpec(keys_pad.shape, lambda i, e: (0, 0, 0))] + piece + piece,
        out_specs=row(D),
        scratch_shapes=[sel32, sel32, sel16, sel16,
                        pltpu.VMEM((ec, tm), F32), pltpu.VMEM((ec, tm), BF16), pltpu.VMEM((tm, D), F32)],
        compiler_params=_cparams(("arbitrary", "arbitrary")),
        name="peer",
    )(x1, xn, qp, keys_pad, *([u_b] * ns), *([v_b] * ns))


def _final_kernel(x_ref, g_ref, o_ref):
    o_ref[...] = _rms_rows(x_ref[...], g_ref[...])


def _final_norm(x, g, tm):
    n, D = x.shape
    return pl.pallas_call(
        _final_kernel,
        out_shape=jax.ShapeDtypeStruct((n, D), F32),
        grid=(n // tm,),
        in_specs=[pl.BlockSpec((tm, D), lambda i: (i, 0)), _full((1, D))],
        out_specs=pl.BlockSpec((tm, D), lambda i: (i, 0)),
        compiler_params=_cparams(("arbitrary",)),
        name="final_norm",
    )(x, g)


def _rope_tables(pos):
    half = 32
    inv = ROPE_THETA ** (-jnp.arange(half, dtype=F32) / half)
    ang = pos.astype(F32)[:, None] * inv[None, :]
    c, s = jnp.cos(ang), jnp.sin(ang)
    cos = jnp.concatenate([c, c, c, c], axis=1)
    sin = jnp.concatenate([-s, s, -s, s], axis=1)
    return cos, sin


def _block_diag_ones(width, group):
    i = np.arange(width) // group
    return jnp.asarray((i[:, None] == i[None, :]).astype(np.float32), dtype=BF16)


def _pick_tile(n, pref):
    t = pref
    while n % t:
        t //= 2
    return t


def kernel(x_prompt, x_sample, cache_k, cache_v, state_conv, page_table, norm1_g, w_in, lam_q1, lam_k1,
           lam_q2, lam_k2, diff_norm_g, sgu_norm_g, w_s, b_s, conv_w, conv_b, conv_norm_g, w_conv_pw,
           w_out, norm2_g, peer_wq, peer_keys, peer_u, peer_v, final_norm_g):
    B, T, D = x_prompt.shape
    DB, S, _ = x_sample.shape
    depth = w_in.shape[0]
    n_phys = cache_k.shape[1]
    n_pages = page_table.shape[1]
    past = n_pages * PAGE
    aw = N_HEADS * HEAD_W
    cwid = sgu_norm_g.shape[1]
    gw = cwid // SGU_GROUPS
    n_exp = peer_u.shape[1]
    assert S == 4 and cwid == conv_norm_g.shape[1] and n_exp == N_KEYS * N_KEYS

    tm_p = _pick_tile(T, TM_PROJ)
    tq = _pick_tile(T, TQ_ATTN)
    tk = tq
    np_tok = B * T
    ns_tok = DB * S
    tm_post_p = _pick_tile(np_tok, TM_TOKENS)
    tm_post_s = _pick_tile(ns_tok, TM_TOKENS)
    ec = EXPERT_BLOCK
    ppb = next(p for p in (16, 8, 4, 2, 1) if n_pages % p == 0)

    cos_p, sin_p = _rope_tables(jnp.arange(T, dtype=jnp.int32))
    pos_s = past + jnp.repeat(jnp.arange(S, dtype=jnp.int32), DB)
    cos_s, sin_s = _rope_tables(pos_s)
    bd = _block_diag_ones(cwid, gw)

    cache_k2 = cache_k.reshape(depth * n_phys * PAGE * N_HEADS, HEAD_W)
    cache_v2 = cache_v.reshape(depth * n_phys * PAGE * N_HEADS, HEAD_W)
    pt_flat = page_table.reshape(-1)

    xs = jnp.transpose(x_sample, (1, 0, 2)).reshape(ns_tok, D)
    xp = x_prompt

    half = (np.arange(HEAD_W) // 64)[None, :] == np.arange(2)[:, None]
    qmask = jnp.asarray(half.astype(np.float32))[None, None, :, None, :]

    kp_l, vp_l, ks_l, vs_l, cp_l, cs_l, gs_l = [], [], [], [], [], [], []
    for l in range(depth):
        lam_init = 0.8 - 0.6 * math.exp(-0.3 * l)
        win_b = w_in[l].astype(BF16)
        wout_b = w_out[l].astype(BF16)
        wq_b = peer_wq[l].astype(BF16)
        wpw_b = w_conv_pw[l].astype(BF16)
        u_b = peer_u[l].astype(BF16)
        v_b = peer_v[l].astype(BF16)
        hw = peer_keys.shape[-1]
        kz = jnp.zeros(peer_keys.shape[1:4] + (LANES - hw,), F32)
        keys_pad = jnp.stack([jnp.concatenate([peer_keys[l, :, 0], kz[:, 0]], axis=-1),
                              jnp.concatenate([kz[:, 1], peer_keys[l, :, 1]], axis=-1)], axis=1)
        keys_pad = keys_pad.reshape(2 * PEER_HEADS, N_KEYS, LANES).astype(BF16)
        lam4 = jnp.stack([lam_q1[l], lam_k1[l], lam_q2[l], lam_k2[l]])
        g1 = norm1_g[l][None, :]
        g2 = norm2_g[l][None, :]
        gd = diff_norm_g[l][None, :]
        sgug = sgu_norm_g[l][None, :]
        cb = conv_b[l][None, :]
        cg = conv_norm_g[l][None, :]
        bs_full = jnp.repeat(b_s[l].T, gw, axis=1)
        wsx = jnp.repeat(jnp.transpose(w_s[l][:, :S, :S], (1, 2, 0)), gw, axis=2)
        bsx = bs_full[:S]

        qa, qb, kf, kb, vf, vb, osg, oc, crow = _proj_prompt(
            xp, g1, win_b, cos_p, sin_p, sgug, w_s[l], bs_full, conv_w[l], cb, cg, wpw_b, bd, tm_p)
        oa = _attn_prompt(qa, qb, kb, vb, lam4, gd, lam_init, tq, tk)
        x1, xn, qp = _post(xp.reshape(np_tok, D), oa.reshape(np_tok, aw), osg.reshape(np_tok, cwid),
                           oc.reshape(np_tok, cwid), wout_b, g2, wq_b, tm_post_p)
        xp = _peer(x1, xn, qp, keys_pad, u_b, v_b, tm_post_p, ec).reshape(B, T, D)
        kp_l.append(kf.reshape(B, T, N_HEADS, HEAD_W))
        vp_l.append(vf.reshape(B, T, N_HEADS, HEAD_W))
        cp_l.append(crow)

        prefix = jnp.transpose(state_conv[l], (1, 0, 2))
        qs, ksf, vsf, osg_s, oc_s, h_s, vn_s = _proj_sample(
            xs, g1, win_b, cos_s, sin_s, sgug, wsx, bsx, prefix, conv_w[l], cb, cg, wpw_b, bd)
        to_b = lambda a: jnp.transpose(a.reshape(S, DB, a.shape[-1]), (1, 0, 2))
        q_b, k_b, v_b2 = to_b(qs), to_b(ksf), to_b(vsf)
        heads = lambda a: jnp.transpose(a.reshape(DB, S, N_HEADS, HEAD_W), (0, 2, 1, 3))
        q32 = (heads(q_b)[:, :, None] * qmask).reshape(DB, 2 * N_HEADS * S, HEAD_W)
        rep = lambda a: jnp.repeat(a.reshape(DB, S, N_HEADS, HEAD_W), 2 * S, axis=2)
        oa_s32 = _attn_sample(pt_flat, q32, rep(k_b), rep(v_b2), cache_k2, cache_v2, lam4, gd, lam_init,
                              l * n_phys, n_pages, ppb)
        oa_s = oa_s32.reshape(DB, N_HEADS, 2, S, HEAD_W)[:, :, 0]
        oa_s = jnp.transpose(oa_s, (2, 0, 1, 3)).reshape(ns_tok, aw)
        x1s, xns, qps = _post(xs, oa_s, osg_s, oc_s, wout_b, g2, wq_b, tm_post_s)
        xs = _peer(x1s, xns, qps, keys_pad, u_b, v_b, tm_post_s, ec)
        ks_l.append(k_b.reshape(DB, S, N_HEADS, HEAD_W))
        vs_l.append(v_b2.reshape(DB, S, N_HEADS, HEAD_W))
        cs_l.append(jnp.concatenate([state_conv[l][:, S:], to_b(h_s)], axis=1))
        gs_l.append(to_b(vn_s))

    fg = final_norm_g[None, :]
    y_prompt = _final_norm(xp.reshape(np_tok, D), fg, tm_post_p).reshape(B, T, D)
    y_sample = jnp.transpose(_final_norm(xs, fg, tm_post_s).reshape(S, DB, D), (1, 0, 2))
    return (y_prompt, y_sample, jnp.stack(kp_l), jnp.stack(vp_l), jnp.stack(ks_l), jnp.stack(vs_l),
            jnp.stack(cp_l), jnp.stack(cs_l), jnp.stack(gs_l))
```

```python
import functools
import math

import numpy as np
import jax
import jax.numpy as jnp
from jax import lax
from jax.experimental import pallas as pl
from jax.experimental.pallas import tpu as pltpu

F32 = jnp.float32
BF16 = jnp.bfloat16

LANES = 128
SUBLANES = 8
EPS = 1e-6
NEG_INF = -1e30
ROPE_THETA = 10000.0

N_HEADS = 4
HEAD_W = 128
SGU_GROUPS = 4
CONV_GROUPS = 4
CONV_K = 31
CHUNK = 128
PAGE = 128
PEER_HEADS = 8
N_KEYS = 128
TOPK = 16

VMEM_LIMIT = 56 * 1024 * 1024

TM_PROJ = 512
TQ_ATTN = 1024
TK_ATTN = 2048
ATTN_ROW_PARTS = 4
TM_TOKENS = 512
EXPERT_BLOCK = 2048


def _cparams(sem):
    return pltpu.CompilerParams(dimension_semantics=sem, vmem_limit_bytes=VMEM_LIMIT)


def _full(shape):
    n = len(shape)
    return pl.BlockSpec(shape, lambda *_: (0,) * n)


def _sigmoid(x):
    return 1.0 / (1.0 + jnp.exp(-x))


def _rms_rows(x, g):
    return x * lax.rsqrt(jnp.mean(x * x, axis=-1, keepdims=True) + EPS) * g


def _group_mean_sq(x, bd, group):
    x2 = x * x
    hi = x2.astype(BF16)
    lo = (x2 - hi.astype(F32)).astype(BF16)
    s = jnp.dot(hi, bd, preferred_element_type=F32) + jnp.dot(lo, bd, preferred_element_type=F32)
    return s * (1.0 / group)


def _rope_block(xh, cos, sin, lo_half):
    sw = jnp.where(lo_half, pltpu.roll(xh, LANES - 32, 1), pltpu.roll(xh, 32, 1))
    return xh * cos + sw * sin


def _lam_from(lam4):
    a = jnp.sum(lam4[0:1, :] * lam4[1:2, :], axis=-1, keepdims=True)
    b = jnp.sum(lam4[2:3, :] * lam4[3:4, :], axis=-1, keepdims=True)
    return jnp.exp(a) - jnp.exp(b)


def _proj_p_kernel(x_ref, g1_ref, win_ref, cos_ref, sin_ref, sgug_ref, ws_ref, bs_ref,
                   cw_ref, cb_ref, cg_ref, wpw_ref, bd_ref,
                   qa_ref, qb_ref, kf_ref, kb_ref, vf_ref, vb_ref, osgu_ref, oconv_ref, crow_ref,
                   hbuf):
    t = pl.program_id(1)
    tm = x_ref.shape[0]
    x = x_ref[...]
    hn = _rms_rows(x, g1_ref[...]).astype(BF16)
    p = jnp.dot(hn, win_ref[...], preferred_element_type=F32)
    aw = N_HEADS * HEAD_W
    q = p[:, 0:aw]
    k = p[:, aw:2 * aw]
    v = p[:, 2 * aw:3 * aw]
    o = 3 * aw
    cw = osgu_ref.shape[1]
    su = p[:, o:o + cw]
    sv = p[:, o + cw:o + 2 * cw]
    ca = p[:, o + 2 * cw:o + 3 * cw]
    cg = p[:, o + 3 * cw:o + 4 * cw]

    cos = cos_ref[...]
    sin = sin_ref[...]
    lane = lax.broadcasted_iota(jnp.int32, (tm, LANES), 1)
    lo_half = (lane & 32) == 0
    first = (lane & 64) == 0
    for h in range(N_HEADS):
        sl = slice(h * HEAD_W, (h + 1) * HEAD_W)
        qr = _rope_block(q[:, sl], cos, sin, lo_half) * 0.125
        qa_ref[:, sl] = jnp.where(first, qr, 0.0).astype(BF16)
        qb_ref[:, sl] = jnp.where(first, 0.0, qr).astype(BF16)
        kr = _rope_block(k[:, sl], cos, sin, lo_half)
        kf_ref[:, sl] = kr
        kb_ref[:, sl] = kr.astype(BF16)
    vf_ref[...] = v
    vb_ref[...] = v.astype(BF16)

    bd = bd_ref[...]
    u = jax.nn.gelu(su)
    vg = jax.nn.gelu(sv)
    vn = vg * lax.rsqrt(_group_mean_sq(vg, bd, cw // SGU_GROUPS) + EPS) * sgug_ref[...]
    vnb = vn.astype(BF16)
    r_i = lax.broadcasted_iota(jnp.int32, (CHUNK, CHUNK), 0)
    c_i = lax.broadcasted_iota(jnp.int32, (CHUNK, CHUNK), 1)
    tril = r_i >= c_i
    ws = [jnp.where(tril, ws_ref[g], 0.0).astype(BF16) for g in range(SGU_GROUPS)]
    left = c_i < 64
    bias = bs_ref[...]
    for c in range(tm // CHUNK):
        rows = slice(c * CHUNK, (c + 1) * CHUNK)
        parts = []
        for gp in range(SGU_GROUPS // 2):
            r = vnb[rows, gp * LANES:(gp + 1) * LANES]
            parts.append(jnp.where(left,
                                   jnp.dot(ws[2 * gp], r, preferred_element_type=F32),
                                   jnp.dot(ws[2 * gp + 1], r, preferred_element_type=F32)))
        mixed = jnp.concatenate(parts, axis=1) + bias
        osgu_ref[rows, :] = (u[rows, :] * mixed).astype(BF16)

    hg = ca * _sigmoid(cg)

    @pl.when(t == 0)
    def _():
        hbuf[0:32, :] = jnp.zeros((32, cw), F32)

    hbuf[32:32 + tm, :] = hg
    y = jnp.zeros((tm, cw), F32) + cb_ref[...]
    for j in range(CONV_K):
        y = y + hbuf[pl.ds(2 + j, tm), :] * cw_ref[j:j + 1, :]
    yn = y * lax.rsqrt(_group_mean_sq(y, bd, cw // CONV_GROUPS) + EPS) * cg_ref[...]
    ys = yn * _sigmoid(yn)
    oconv_ref[...] = jnp.dot(ys.astype(BF16), wpw_ref[...], preferred_element_type=F32).astype(BF16)
    crow_ref[...] = hbuf[pl.ds(tm + 2, CONV_K - 1), :]
    hbuf[0:32, :] = hbuf[tm:tm + 32, :]


def _proj_prompt(x, g1, win_b, cos, sin, sgug, ws, bs_full, cw, cb, cg, wpw_b, bd, tm):
    B, T, D = x.shape
    PW = win_b.shape[1]
    aw = N_HEADS * HEAD_W
    cwid = sgug.shape[1]
    row = lambda w: pl.BlockSpec((None, tm, w), lambda b, t: (b, t, 0))
    outs = (
        jax.ShapeDtypeStruct((B, T, aw), BF16), jax.ShapeDtypeStruct((B, T, aw), BF16),
        jax.ShapeDtypeStruct((B, T, aw), F32), jax.ShapeDtypeStruct((B, T, aw), BF16),
        jax.ShapeDtypeStruct((B, T, aw), F32), jax.ShapeDtypeStruct((B, T, aw), BF16),
        jax.ShapeDtypeStruct((B, T, cwid), BF16), jax.ShapeDtypeStruct((B, T, cwid), BF16),
        jax.ShapeDtypeStruct((B, CONV_K - 1, cwid), F32),
    )
    return pl.pallas_call(
        _proj_p_kernel,
        out_shape=outs,
        grid=(B, T // tm),
        in_specs=[row(D), _full((1, D)), _full((D, PW)),
                  pl.BlockSpec((tm, LANES), lambda b, t: (t, 0)),
                  pl.BlockSpec((tm, LANES), lambda b, t: (t, 0)),
                  _full((1, cwid)), _full(ws.shape), _full(bs_full.shape),
                  _full(cw.shape), _full((1, cwid)), _full((1, cwid)), _full(wpw_b.shape), _full(bd.shape)],
        out_specs=(row(aw), row(aw), row(aw), row(aw), row(aw), row(aw), row(cwid), row(cwid),
                   pl.BlockSpec((None, CONV_K - 1, cwid), lambda b, t: (b, 0, 0))),
        scratch_shapes=[pltpu.VMEM((tm + 32, cwid), F32)],
        compiler_params=_cparams(("arbitrary", "arbitrary")),
        name="proj_prompt",
    )(x, g1, win_b, cos, sin, sgug, ws, bs_full, cw, cb, cg, wpw_b, bd)


def _proj_s_kernel(x_ref, g1_ref, win_ref, cos_ref, sin_ref, sgug_ref, wsx_ref, bsx_ref,
                   pre_ref, cw_ref, cb_ref, cg_ref, wpw_ref, bd_ref,
                   q_ref, kf_ref, vf_ref, osgu_ref, oconv_ref, h_ref, vn_ref):
    n = x_ref.shape[0]
    nb = pre_ref.shape[1]
    S = n // nb
    x = x_ref[...]
    hn = _rms_rows(x, g1_ref[...]).astype(BF16)
    p = jnp.dot(hn, win_ref[...], preferred_element_type=F32)
    aw = N_HEADS * HEAD_W
    q = p[:, 0:aw]
    k = p[:, aw:2 * aw]
    o = 3 * aw
    cw = osgu_ref.shape[1]
    su = p[:, o:o + cw]
    sv = p[:, o + cw:o + 2 * cw]
    ca = p[:, o + 2 * cw:o + 3 * cw]
    cg = p[:, o + 3 * cw:o + 4 * cw]
    cos = cos_ref[...]
    sin = sin_ref[...]
    lane = lax.broadcasted_iota(jnp.int32, (n, LANES), 1)
    lo_half = (lane & 32) == 0
    for h in range(N_HEADS):
        sl = slice(h * HEAD_W, (h + 1) * HEAD_W)
        q_ref[:, sl] = _rope_block(q[:, sl], cos, sin, lo_half) * 0.125
        kf_ref[:, sl] = _rope_block(k[:, sl], cos, sin, lo_half)
    vf_ref[...] = p[:, 2 * aw:3 * aw]

    bd = bd_ref[...]
    u = jax.nn.gelu(su)
    vg = jax.nn.gelu(sv)
    vn = vg * lax.rsqrt(_group_mean_sq(vg, bd, cw // SGU_GROUPS) + EPS) * sgug_ref[...]
    vn_ref[...] = vn
    for t in range(S):
        mixed = jnp.zeros((nb, cw), F32) + bsx_ref[t:t + 1, :]
        for s in range(t + 1):
            mixed = mixed + wsx_ref[t, s:s + 1, :] * vn[s * nb:(s + 1) * nb, :]
        osgu_ref[t * nb:(t + 1) * nb, :] = (u[t * nb:(t + 1) * nb, :] * mixed).astype(BF16)

    hg = ca * _sigmoid(cg)
    h_ref[...] = hg
    npre = CONV_K - 1
    ys = []
    for t in range(S):
        y = jnp.zeros((nb, cw), F32) + cb_ref[...]
        for j in range(CONV_K):
            kk = t + j
            src = pre_ref[kk] if kk < npre else hg[(kk - npre) * nb:(kk - npre + 1) * nb, :]
            y = y + src * cw_ref[j:j + 1, :]
        ys.append(y)
    y = jnp.concatenate(ys, axis=0)
    yn = y * lax.rsqrt(_group_mean_sq(y, bd, cw // CONV_GROUPS) + EPS) * cg_ref[...]
    ysw = yn * _sigmoid(yn)
    oconv_ref[...] = jnp.dot(ysw.astype(BF16), wpw_ref[...], preferred_element_type=F32).astype(BF16)


def _proj_sample(x, g1, win_b, cos, sin, sgug, wsx, bsx, prefix, cw, cb, cg, wpw_b, bd):
    n, D = x.shape
    aw = N_HEADS * HEAD_W
    cwid = sgug.shape[1]
    args = (x, g1, win_b, cos, sin, sgug, wsx, bsx, prefix, cw, cb, cg, wpw_b, bd)
    outs = (jax.ShapeDtypeStruct((n, aw), F32), jax.ShapeDtypeStruct((n, aw), F32),
            jax.ShapeDtypeStruct((n, aw), F32),
            jax.ShapeDtypeStruct((n, cwid), BF16), jax.ShapeDtypeStruct((n, cwid), BF16),
            jax.ShapeDtypeStruct((n, cwid), F32), jax.ShapeDtypeStruct((n, cwid), F32))
    return pl.pallas_call(
        _proj_s_kernel,
        out_shape=outs,
        grid=(1,),
        in_specs=[_full(a.shape) for a in args],
        out_specs=tuple(_full(o.shape) for o in outs),
        compiler_params=_cparams(("arbitrary",)),
        name="proj_sample",
    )(*args)


def _attn_p_kernel(qt_ref, kt_ref, lam4_ref, g_ref, qa_ref, qb_ref, k_ref, v_ref, o_ref,
                   m0, l0, a0, m1, l1, a1, *, lam_init):
    s_id = pl.program_id(2)
    qi = qt_ref[s_id]
    ki = kt_ref[s_id]
    tq = qa_ref.shape[0]
    tk = k_ref.shape[0]

    @pl.when(ki == 0)
    def _():
        for m, l, a in ((m0, l0, a0), (m1, l1, a1)):
            m[...] = jnp.full(m.shape, NEG_INF, F32)
            l[...] = jnp.zeros(l.shape, F32)
            a[...] = jnp.zeros(a.shape, F32)

    kk = k_ref[...]
    vv = v_ref[...]
    comps = ((qa_ref, m0, l0, a0), (qb_ref, m1, l1, a1))
    dn = (((1,), (1,)), ((), ()))
    rq = tq // ATTN_ROW_PARTS
    for part in range(ATTN_ROW_PARTS):
        rows = slice(part * rq, (part + 1) * rq)
        qpos = qi * tq + part * rq + lax.broadcasted_iota(jnp.int32, (rq, tk), 0)
        kpos = ki * tk + lax.broadcasted_iota(jnp.int32, (rq, tk), 1)
        keep = kpos <= qpos
        ss = [jnp.where(keep, lax.dot_general(q_ref[rows, :], kk, dn, preferred_element_type=F32), NEG_INF)
              for q_ref, _, _, _ in comps]
        ps = []
        for s, (_, m, l, _) in zip(ss, comps):
            m_old = m[rows, :]
            m_new = jnp.maximum(m_old, jnp.max(s, axis=-1, keepdims=True))
            alpha = jnp.exp(m_old - m_new)
            p = jnp.exp(s - m_new)
            l[rows, :] = alpha * l[rows, :] + jnp.sum(p, axis=-1, keepdims=True)
            m[rows, :] = m_new
            ps.append((p.astype(BF16), alpha))
        for (p, alpha), (_, _, _, a) in zip(ps, comps):
            a[rows, :] = alpha * a[rows, :] + jnp.dot(p, vv, preferred_element_type=F32)

    @pl.when(ki == (qi * tq + tq - 1) // tk)
    def _():
        lam = _lam_from(lam4_ref[...]) + lam_init
        o = a0[...] / l0[...] - lam * (a1[...] / l1[...])
        o_ref[...] = (_rms_rows(o, g_ref[...]) * (1.0 - lam_init)).astype(o_ref.dtype)


def _attn_prompt(qa, qb, kb, vb, lam4, g, lam_init, tq, tk):
    B, T, aw = qa.shape
    nq = T // tq
    qs, ks = [], []
    for i in range(nq):
        last = (i * tq + tq - 1) // tk
        for j in range(last + 1):
            qs.append(i)
            ks.append(j)
    qt = jnp.asarray(np.array(qs, np.int32))
    kt = jnp.asarray(np.array(ks, np.int32))
    qspec = pl.BlockSpec((None, tq, HEAD_W), lambda b, h, s, qt, kt: (b, qt[s], h))
    kspec = pl.BlockSpec((None, tk, HEAD_W), lambda b, h, s, qt, kt: (b, kt[s], h))
    grid_spec = pltpu.PrefetchScalarGridSpec(
        num_scalar_prefetch=2,
        grid=(B, N_HEADS, len(qs)),
        in_specs=[pl.BlockSpec(lam4.shape, lambda b, h, s, qt, kt: (0, 0)),
                  pl.BlockSpec(g.shape, lambda b, h, s, qt, kt: (0, 0)),
                  qspec, qspec, kspec, kspec],
        out_specs=qspec,
        scratch_shapes=[pltpu.VMEM((tq, 1), F32), pltpu.VMEM((tq, 1), F32), pltpu.VMEM((tq, HEAD_W), F32),
                        pltpu.VMEM((tq, 1), F32), pltpu.VMEM((tq, 1), F32), pltpu.VMEM((tq, HEAD_W), F32)],
    )
    return pl.pallas_call(
        functools.partial(_attn_p_kernel, lam_init=lam_init),
        out_shape=jax.ShapeDtypeStruct((B, T, aw), BF16),
        grid_spec=grid_spec,
        compiler_params=_cparams(("arbitrary", "arbitrary", "arbitrary")),
        name="attn_prompt",
    )(qt, kt, lam4, g, qa, qb, kb, vb)


def _attn_s_kernel(pt_ref, lam4_ref, g_ref, q_ref, knx_ref, vnx_ref, *rest, lam_init, ppb, n_new):
    k_refs = rest[:ppb]
    v_refs = rest[ppb:2 * ppb]
    o_ref = rest[2 * ppb]
    m_sc, l_sc, a_sc = rest[2 * ppb + 1:]
    j = pl.program_id(1)
    nj = pl.num_programs(1)

    @pl.when(j == 0)
    def _():
        m_sc[...] = jnp.full(m_sc.shape, NEG_INF, F32)
        l_sc[...] = jnp.zeros(l_sc.shape, F32)
        a_sc[...] = jnp.zeros(a_sc.shape, F32)

    q = q_ref[...]
    qb = q.astype(BF16)
    rows = q.shape[0]
    ncol = k_refs[0].shape[0]
    col = lax.broadcasted_iota(jnp.int32, (rows, ncol), 1)
    row = lax.broadcasted_iota(jnp.int32, (rows, ncol), 0)
    own = (col & (N_HEADS - 1)) == (row >> 3)
    ss = []
    for kr in k_refs:
        kk = kr[...].astype(BF16)
        s = lax.dot_general(qb, kk, (((1,), (1,)), ((), ())), preferred_element_type=F32)
        ss.append(jnp.where(own, s, NEG_INF))
    m_old = m_sc[...]
    m_new = m_old
    for s in ss:
        m_new = jnp.maximum(m_new, jnp.max(s, axis=-1, keepdims=True))
    alpha = jnp.exp(m_old - m_new)
    l = alpha * l_sc[...]
    acc = alpha * a_sc[...]
    for s, vr in zip(ss, v_refs):
        p = jnp.exp(s - m_new)
        l = l + jnp.sum(p, axis=-1, keepdims=True)
        acc = acc + jnp.dot(p.astype(BF16), vr[...].astype(BF16), preferred_element_type=F32)
    l_sc[...] = l
    a_sc[...] = acc
    m_sc[...] = m_new

    @pl.when(j == nj - 1)
    def _():
        tok = lax.broadcasted_iota(jnp.int32, (rows, 1), 0) & (n_new - 1)
        sn = []
        for c in range(n_new):
            sc = jnp.sum(q * knx_ref[c], axis=-1, keepdims=True)
            sn.append(jnp.where(tok >= c, sc, NEG_INF))
        m_old = m_sc[...]
        m_new = m_old
        for sc in sn:
            m_new = jnp.maximum(m_new, sc)
        alpha = jnp.exp(m_old - m_new)
        l = alpha * l_sc[...]
        acc = alpha * a_sc[...]
        for c, sc in enumerate(sn):
            pc = jnp.exp(sc - m_new)
            l = l + pc
            acc = acc + pc * vnx_ref[c]
        xo = acc / l
        lam = _lam_from(lam4_ref[...]) + lam_init
        o = xo - lam * pltpu.roll(xo, rows - n_new, 0)
        o_ref[...] = _rms_rows(o, g_ref[...]) * (1.0 - lam_init)


def _attn_sample(pt_flat, q32, knx, vnx, cache_k2, cache_v2, lam4, g, lam_init, layer_base, n_pages, ppb):
    nb = q32.shape[0]
    n_new = knx.shape[1]
    steps = n_pages // ppb
    prow = PAGE * N_HEADS

    def page_spec(p):
        return pl.BlockSpec((prow, HEAD_W),
                            lambda b, j, pt, p=p: (layer_base + pt[b * n_pages + j * ppb + p], 0))

    grid_spec = pltpu.PrefetchScalarGridSpec(
        num_scalar_prefetch=1,
        grid=(nb, steps),
        in_specs=[pl.BlockSpec(lam4.shape, lambda b, j, pt: (0, 0)),
                  pl.BlockSpec(g.shape, lambda b, j, pt: (0, 0)),
                  pl.BlockSpec((None, 32, HEAD_W), lambda b, j, pt: (b, 0, 0)),
                  pl.BlockSpec((None, n_new, 32, HEAD_W), lambda b, j, pt: (b, 0, 0, 0)),
                  pl.BlockSpec((None, n_new, 32, HEAD_W), lambda b, j, pt: (b, 0, 0, 0))]
                 + [page_spec(p) for p in range(ppb)] + [page_spec(p) for p in range(ppb)],
        out_specs=pl.BlockSpec((None, 32, HEAD_W), lambda b, j, pt: (b, 0, 0)),
        scratch_shapes=[pltpu.VMEM((32, 1), F32), pltpu.VMEM((32, 1), F32), pltpu.VMEM((32, HEAD_W), F32)],
    )
    return pl.pallas_call(
        functools.partial(_attn_s_kernel, lam_init=lam_init, ppb=ppb, n_new=n_new),
        out_shape=jax.ShapeDtypeStruct((nb, 32, HEAD_W), F32),
        grid_spec=grid_spec,
        compiler_params=_cparams(("arbitrary", "arbitrary")),
        name="attn_sample",
    )(pt_flat, lam4, g, q32, knx, vnx, *([cache_k2] * ppb), *([cache_v2] * ppb))


def _post_kernel(x_ref, oa_ref, os_ref, oc_ref, wout_ref, g2_ref, wq_ref, x1_ref, xn_ref, qp_ref):
    aw = oa_ref.shape[1]
    cw = os_ref.shape[1]
    x1 = (x_ref[...]
          + jnp.dot(oa_ref[...].astype(BF16), wout_ref[0:aw, :], preferred_element_type=F32)
          + jnp.dot(os_ref[...], wout_ref[aw:aw + cw, :], preferred_element_type=F32)
          + jnp.dot(oc_ref[...], wout_ref[aw + cw:aw + 2 * cw, :], preferred_element_type=F32))
    x1_ref[...] = x1
    xn = _rms_rows(x1, g2_ref[...]).astype(BF16)
    xn_ref[...] = xn
    qp = jnp.dot(xn, wq_ref[...], preferred_element_type=F32).astype(BF16)
    for h in range(PEER_HEADS):
        qp_ref[h] = qp[:, h * LANES:(h + 1) * LANES]


def _post(x, oa, osg, oc, wout_b, g2, wq_b, tm):
    n, D = x.shape
    row = lambda w: pl.BlockSpec((tm, w), lambda i: (i, 0))
    return pl.pallas_call(
        _post_kernel,
        out_shape=(jax.ShapeDtypeStruct((n, D), F32), jax.ShapeDtypeStruct((n, D), BF16),
                   jax.ShapeDtypeStruct((PEER_HEADS, n, LANES), BF16)),
        grid=(n // tm,),
        in_specs=[row(D), row(oa.shape[1]), row(osg.shape[1]), row(oc.shape[1]),
                  _full(wout_b.shape), _full((1, D)), _full(wq_b.shape)],
        out_specs=(row(D), row(D), pl.BlockSpec((PEER_HEADS, tm, LANES), lambda i: (0, i, 0))),
        compiler_params=_cparams(("arbitrary",)),
        name="post",
    )(x, oa, osg, oc, wout_b, g2, wq_b)


def _sort_pairs(n):
    pairs = []
    p = 1
    while p < n:
        k = p
        while k >= 1:
            for j in range(k % p, n - k, 2 * k):
                for i in range(min(k, n - j - k)):
                    if (i + j) // (2 * p) == (i + j + k) // (2 * p):
                        pairs.append((i + j, i + j + k))
            k //= 2
        p *= 2
    return pairs


def _cmpx(lst, i, j):
    a, b = lst[i], lst[j]
    if b is None:
        return
    if a is None:
        lst[i], lst[j] = b, None
        return
    lst[i], lst[j] = jnp.maximum(a, b), jnp.minimum(a, b)


def _sort_desc(vals):
    n = 1
    while n < len(vals):
        n *= 2
    lst = list(vals) + [None] * (n - len(vals))
    for i, j in _sort_pairs(n):
        _cmpx(lst, i, j)
    return lst


def _top16_rows(sT):
    lst = _sort_desc([sT[SUBLANES * v:SUBLANES * (v + 1), :] for v in range(N_KEYS // SUBLANES)])
    for shift in (4, 2, 1):
        other = [pltpu.roll(x, shift, 0) for x in lst]
        lst = [jnp.maximum(lst[i], other[TOPK - 1 - i]) for i in range(TOPK)]
        d = TOPK // 2
        while d >= 1:
            for i in range(TOPK):
                if (i & d) == 0:
                    _cmpx(lst, i, i + d)
            d //= 2
    return lst


def _peer_select(q, k1, k2):
    dn = (((1,), (1,)), ((), ()))
    s1 = lax.dot_general(k1, q, dn, preferred_element_type=F32)
    s2 = lax.dot_general(k2, q, dn, preferred_element_type=F32)
    a = _top16_rows(s1)
    b = _top16_rows(s2)
    cand = {}
    for r in range(TOPK):
        for c in range(TOPK // (r + 1)):
            cand[(r, c)] = a[r] + b[c]
    tau = _sort_desc(list(cand.values()))[TOPK - 1]
    m = cand[(0, 0)]
    z = jnp.zeros(tau.shape, F32)
    len_r = []
    for r in range(TOPK):
        cnt = jnp.zeros(tau.shape, F32)
        for c in range(TOPK // (r + 1)):
            sel = cand[(r, c)] >= tau
            cnt = cnt + jnp.where(sel, 1.0, 0.0)
            z = z + jnp.where(sel, jnp.exp(cand[(r, c)] - m), 0.0)
        len_r.append(cnt)
    zinv = 1.0 / z
    nv = N_KEYS // SUBLANES
    len1, e1, rank2, e2 = [], [], [], []
    for v in range(nv):
        rows = s1[SUBLANES * v:SUBLANES * (v + 1), :]
        ln = jnp.zeros(tau.shape, F32)
        for r in range(TOPK - 1, -1, -1):
            ln = jnp.where(rows >= a[r], len_r[r], ln)
        len1.append(ln)
        e1.append(jnp.exp(rows - a[0]))
        rows2 = s2[SUBLANES * v:SUBLANES * (v + 1), :]
        rk = jnp.full(tau.shape, float(TOPK), F32)
        for c in range(TOPK - 1, -1, -1):
            rk = jnp.where(rows2 >= b[c], float(c), rk)
        rank2.append(rk)
        e2.append(jnp.exp(rows2 - b[0]) * zinv)
    cat = lambda xs: jnp.concatenate(xs, axis=0)
    return cat(len1), cat(e1), cat(rank2).astype(BF16), cat(e2).astype(BF16)


def _peer_kernel(x1_ref, xn_ref, qp_ref, keys_ref, u_ref, v_ref, o_ref,
                 ln_sc, e1_sc, r2_sc, e2_sc, at_sc, ht_sc, acc_sc):
    e = pl.program_id(1)
    ne = pl.num_programs(1)
    tm = xn_ref.shape[0]
    ec = u_ref.shape[0]
    ntb = tm // LANES
    ni = ec // N_KEYS

    @pl.when(e == 0)
    def _():
        acc_sc[...] = jnp.zeros(acc_sc.shape, F32)

        def body(it, carry):
            h = it // ntb
            tb = it % ntb
            q = qp_ref[h, pl.ds(pl.multiple_of(tb * LANES, LANES), LANES), :]
            ln, e1, r2, e2 = _peer_select(q, keys_ref[2 * h], keys_ref[2 * h + 1])
            ln_sc[h, tb] = ln
            e1_sc[h, tb] = e1
            r2_sc[h, tb] = r2
            e2_sc[h, tb] = e2
            return carry

        lax.fori_loop(0, PEER_HEADS * ntb, body, 0)

    at_sc[...] = lax.dot_general(u_ref[...], xn_ref[...], (((1,), (1,)), ((), ())),
                                 preferred_element_type=F32)

    def dense(il, carry):
        ig = e * ni + il
        r0 = pl.multiple_of(il * N_KEYS, N_KEYS)
        for tb in range(ntb):
            cols = slice(tb * LANES, (tb + 1) * LANES)
            act = jax.nn.gelu(at_sc[pl.ds(r0, N_KEYS), cols].astype(BF16))
            gate = jnp.zeros((N_KEYS, LANES), BF16)
            for h in range(PEER_HEADS):
                ln = ln_sc[h, tb, pl.ds(ig, 1), :].astype(BF16)
                e1 = e1_sc[h, tb, pl.ds(ig, 1), :].astype(BF16)
                gate = gate + jnp.where(r2_sc[h, tb] < ln, e2_sc[h, tb], jnp.zeros((), BF16)) * e1
            ht_sc[pl.ds(r0, N_KEYS), cols] = act * gate
        return carry

    lax.fori_loop(0, ni, dense, 0)
    acc_sc[...] += lax.dot_general(ht_sc[...], v_ref[...], (((0,), (0,)), ((), ())),
                                   preferred_element_type=F32)

    @pl.when(e == ne - 1)
    def _():
        o_ref[...] = x1_ref[...] + acc_sc[...]


def _peer(x1, xn, qp, keys_pad, u_b, v_b, tm, ec):
    n, D = x1.shape
    ne = u_b.shape[0] // ec
    ntb = tm // LANES
    row = lambda w: pl.BlockSpec((tm, w), lambda i, e: (i, 0))
    sel32 = pltpu.VMEM((PEER_HEADS, ntb, N_KEYS, LANES), F32)
    sel16 = pltpu.VMEM((PEER_HEADS, ntb, N_KEYS, LANES), BF16)
    return pl.pallas_call(
        _peer_kernel,
        out_shape=jax.ShapeDtypeStruct((n, D), F32),
        grid=(n // tm, ne),
        in_specs=[row(D), row(D),
                  pl.BlockSpec((PEER_HEADS, tm, LANES), lambda i, e: (0, i, 0)),
                  pl.BlockSpec(keys_pad.shape, lambda i, e: (0, 0, 0)),
                  pl.BlockSpec((ec, D), lambda i, e: (e, 0)),
                  pl.BlockSpec((ec, D), lambda i, e: (e, 0))],
        out_specs=row(D),
        scratch_shapes=[sel32, sel32, sel16, sel16,
                        pltpu.VMEM((ec, tm), F32), pltpu.VMEM((ec, tm), BF16), pltpu.VMEM((tm, D), F32)],
        compiler_params=_cparams(("arbitrary", "arbitrary")),
        name="peer",
    )(x1, xn, qp, keys_pad, u_b, v_b)


def _final_kernel(x_ref, g_ref, o_ref):
    o_ref[...] = _rms_rows(x_ref[...], g_ref[...])


def _final_norm(x, g, tm):
    n, D = x.shape
    return pl.pallas_call(
        _final_kernel,
        out_shape=jax.ShapeDtypeStruct((n, D), F32),
        grid=(n // tm,),
        in_specs=[pl.BlockSpec((tm, D), lambda i: (i, 0)), _full((1, D))],
        out_specs=pl.BlockSpec((tm, D), lambda i: (i, 0)),
        compiler_params=_cparams(("arbitrary",)),
        name="final_norm",
    )(x, g)


def _rope_tables(pos):
    half = 32
    inv = ROPE_THETA ** (-jnp.arange(half, dtype=F32) / half)
    ang = pos.astype(F32)[:, None] * inv[None, :]
    c, s = jnp.cos(ang), jnp.sin(ang)
    cos = jnp.concatenate([c, c, c, c], axis=1)
    sin = jnp.concatenate([-s, s, -s, s], axis=1)
    return cos, sin


def _block_diag_ones(width, group):
    i = np.arange(width) // group
    return jnp.asarray((i[:, None] == i[None, :]).astype(np.float32), dtype=BF16)


def _pick_tile(n, pref):
    t = pref
    while n % t:
        t //= 2
    return t


def kernel(x_prompt, x_sample, cache_k, cache_v, state_conv, page_table, norm1_g, w_in, lam_q1, lam_k1,
           lam_q2, lam_k2, diff_norm_g, sgu_norm_g, w_s, b_s, conv_w, conv_b, conv_norm_g, w_conv_pw,
           w_out, norm2_g, peer_wq, peer_keys, peer_u, peer_v, final_norm_g):
    B, T, D = x_prompt.shape
    DB, S, _ = x_sample.shape
    depth = w_in.shape[0]
    n_phys = cache_k.shape[1]
    n_pages = page_table.shape[1]
    past = n_pages * PAGE
    aw = N_HEADS * HEAD_W
    cwid = sgu_norm_g.shape[1]
    gw = cwid // SGU_GROUPS
    n_exp = peer_u.shape[1]
    assert S == 4 and cwid == conv_norm_g.shape[1] and n_exp == N_KEYS * N_KEYS

    tm_p = _pick_tile(T, TM_PROJ)
    tq = _pick_tile(T, TQ_ATTN)
    tk = _pick_tile(T, TK_ATTN)
    np_tok = B * T
    ns_tok = DB * S
    tm_post_p = _pick_tile(np_tok, TM_TOKENS)
    tm_post_s = _pick_tile(ns_tok, TM_TOKENS)
    ec = EXPERT_BLOCK
    ppb = next(p for p in (16, 8, 4, 2, 1) if n_pages % p == 0)

    cos_p, sin_p = _rope_tables(jnp.arange(T, dtype=jnp.int32))
    pos_s = past + jnp.repeat(jnp.arange(S, dtype=jnp.int32), DB)
    cos_s, sin_s = _rope_tables(pos_s)
    bd = _block_diag_ones(cwid, gw)

    cache_k2 = cache_k.reshape(depth * n_phys * PAGE * N_HEADS, HEAD_W)
    cache_v2 = cache_v.reshape(depth * n_phys * PAGE * N_HEADS, HEAD_W)
    pt_flat = page_table.reshape(-1)

    xs = jnp.transpose(x_sample, (1, 0, 2)).reshape(ns_tok, D)
    xp = x_prompt

    half = (np.arange(HEAD_W) // 64)[None, :] == np.arange(2)[:, None]
    qmask = jnp.asarray(half.astype(np.float32))[None, None, :, None, :]

    kp_l, vp_l, ks_l, vs_l, cp_l, cs_l, gs_l = [], [], [], [], [], [], []
    for l in range(depth):
        lam_init = 0.8 - 0.6 * math.exp(-0.3 * l)
        win_b = w_in[l].astype(BF16)
        wout_b = w_out[l].astype(BF16)
        wq_b = peer_wq[l].astype(BF16)
        wpw_b = w_conv_pw[l].astype(BF16)
        u_b = peer_u[l].astype(BF16)
        v_b = peer_v[l].astype(BF16)
        hw = peer_keys.shape[-1]
        kz = jnp.zeros(peer_keys.shape[1:4] + (LANES - hw,), F32)
        keys_pad = jnp.stack([jnp.concatenate([peer_keys[l, :, 0], kz[:, 0]], axis=-1),
                              jnp.concatenate([kz[:, 1], peer_keys[l, :, 1]], axis=-1)], axis=1)
        keys_pad = keys_pad.reshape(2 * PEER_HEADS, N_KEYS, LANES).astype(BF16)
        lam4 = jnp.stack([lam_q1[l], lam_k1[l], lam_q2[l], lam_k2[l]])
        g1 = norm1_g[l][None, :]
        g2 = norm2_g[l][None, :]
        gd = diff_norm_g[l][None, :]
        sgug = sgu_norm_g[l][None, :]
        cb = conv_b[l][None, :]
        cg = conv_norm_g[l][None, :]
        bs_full = jnp.repeat(b_s[l].T, gw, axis=1)
        wsx = jnp.repeat(jnp.transpose(w_s[l][:, :S, :S], (1, 2, 0)), gw, axis=2)
        bsx = bs_full[:S]

        qa, qb, kf, kb, vf, vb, osg, oc, crow = _proj_prompt(
            xp, g1, win_b, cos_p, sin_p, sgug, w_s[l], bs_full, conv_w[l], cb, cg, wpw_b, bd, tm_p)
        oa = _attn_prompt(qa, qb, kb, vb, lam4, gd, lam_init, tq, tk)
        x1, xn, qp = _post(xp.reshape(np_tok, D), oa.reshape(np_tok, aw), osg.reshape(np_tok, cwid),
                           oc.reshape(np_tok, cwid), wout_b, g2, wq_b, tm_post_p)
        xp = _peer(x1, xn, qp, keys_pad, u_b, v_b, tm_post_p, ec).reshape(B, T, D)
        kp_l.append(kf.reshape(B, T, N_HEADS, HEAD_W))
        vp_l.append(vf.reshape(B, T, N_HEADS, HEAD_W))
        cp_l.append(crow)

        prefix = jnp.transpose(state_conv[l], (1, 0, 2))
        qs, ksf, vsf, osg_s, oc_s, h_s, vn_s = _proj_sample(
            xs, g1, win_b, cos_s, sin_s, sgug, wsx, bsx, prefix, conv_w[l], cb, cg, wpw_b, bd)
        to_b = lambda a: jnp.transpose(a.reshape(S, DB, a.shape[-1]), (1, 0, 2))
        q_b, k_b, v_b2 = to_b(qs), to_b(ksf), to_b(vsf)
        heads = lambda a: jnp.transpose(a.reshape(DB, S, N_HEADS, HEAD_W), (0, 2, 1, 3))
        q32 = (heads(q_b)[:, :, None] * qmask).reshape(DB, 2 * N_HEADS * S, HEAD_W)
        rep = lambda a: jnp.repeat(a.reshape(DB, S, N_HEADS, HEAD_W), 2 * S, axis=2)
        oa_s32 = _attn_sample(pt_flat, q32, rep(k_b), rep(v_b2), cache_k2, cache_v2, lam4, gd, lam_init,
                              l * n_phys, n_pages, ppb)
        oa_s = oa_s32.reshape(DB, N_HEADS, 2, S, HEAD_W)[:, :, 0]
        oa_s = jnp.transpose(oa_s, (2, 0, 1, 3)).reshape(ns_tok, aw)
        x1s, xns, qps = _post(xs, oa_s, osg_s, oc_s, wout_b, g2, wq_b, tm_post_s)
        xs = _peer(x1s, xns, qps, keys_pad, u_b, v_b, tm_post_s, ec)
        ks_l.append(k_b.reshape(DB, S, N_HEADS, HEAD_W))
        vs_l.append(v_b2.reshape(DB, S, N_HEADS, HEAD_W))
        cs_l.append(jnp.concatenate([state_conv[l][:, S:], to_b(h_s)], axis=1))
        gs_l.append(to_b(vn_s))

    fg = final_norm_g[None, :]
    y_prompt = _final_norm(xp.reshape(np_tok, D), fg, tm_post_p).reshape(B, T, D)
    y_sample = jnp.transpose(_final_norm(xs, fg, tm_post_s).reshape(S, DB, D), (1, 0, 2))
    return (y_prompt, y_sample, jnp.stack(kp_l), jnp.stack(vp_l), jnp.stack(ks_l), jnp.stack(vs_l),
            jnp.stack(cp_l), jnp.stack(cs_l), jnp.stack(gs_l))
```
